```python
import math
import jax, jax.numpy as jnp
from jax import lax
import numpy as np

D_MODEL = 1024
BATCH = 16
SEQ = 2048
DEPTH = 1

HEAD_DIM = 64
POOL_WINDOWS = (2, 4, 8, 16)
POOL_GROUP = 64
POOL_WIDTH = POOL_GROUP * len(POOL_WINDOWS)
ATTN_CONFIGS = ((128, 1), (512, 4), (2048, 16))
HEADS_PER_GROUP = 4
N_ATTN_HEADS = HEADS_PER_GROUP * len(ATTN_CONFIGS)
ATTN_WIDTH = N_ATTN_HEADS * HEAD_DIM
MIX_WIDTH = POOL_WIDTH + ATTN_WIDTH
IN_PROJ_WIDTH = POOL_WIDTH + 3 * ATTN_WIDTH
ATTN_BLOCK = 64
PEER_HEADS = 8
PEER_N_KEYS = 128
PEER_N_EXPERTS = PEER_N_KEYS * PEER_N_KEYS
PEER_HALF = 128
PEER_QUERY_DIM = 2 * PEER_HALF
PEER_TOPK = 16
PEER_CHUNK = 128
RMS_EPS = 1e-6
NEG_INF = -1e30

kernel_name = "hybrid_pool_dilated_alibi_peer_block"


def rms_norm(x, g):
    xf = x.astype(jnp.float32)
    y = xf * lax.rsqrt(jnp.mean(xf * xf, axis=-1, keepdims=True) + RMS_EPS)
    return (y * g.astype(jnp.float32)).astype(x.dtype)


def alibi_slopes():
    return jnp.asarray(2.0 ** (-8.0 * np.arange(1, N_ATTN_HEADS + 1) / N_ATTN_HEADS), dtype=jnp.float32)


def multiscale_pool(p, w_pool, pool_scale):
    B, S, C = p.shape
    pf = p.astype(jnp.float32)
    csum = jnp.concatenate([jnp.zeros((B, 1, C), jnp.float32), jnp.cumsum(pf, axis=1)], axis=1)
    pos = jnp.arange(S)
    outs = []
    for gi, w in enumerate(POOL_WINDOWS):
        sl = slice(gi * POOL_GROUP, (gi + 1) * POOL_GROUP)
        lo = jnp.clip(pos - w // 2, 0, S)
        hi = jnp.clip(pos + w // 2, 0, S)
        cg = csum[..., sl]
        mean = (cg[:, hi] - cg[:, lo]) / (hi - lo).astype(jnp.float32)[None, :, None]
        outs.append(mean - pf[..., sl])
    pooled = jnp.stack(outs, axis=2)
    mixed = jnp.einsum('bsgc,gcd->bsgd', pooled, w_pool.astype(jnp.float32))
    return (mixed.reshape(B, S, C) * pool_scale.astype(jnp.float32)).astype(p.dtype)


def dilated_window_attention(q, k, v, window, dilation, slopes):
    B, S, H, Dh = q.shape
    L = S // dilation
    half = window // (2 * dilation)
    nb = -(-L // ATTN_BLOCK)
    Lp = nb * ATTN_BLOCK

    def to_sub(t):
        return t.reshape(B, L, dilation, H, Dh).transpose(0, 2, 1, 3, 4)

    qs = jnp.pad(to_sub(q), ((0, 0), (0, 0), (0, Lp - L), (0, 0), (0, 0)))
    qb = qs.reshape(B, dilation, nb, ATTN_BLOCK, H, Dh)

    def key_blocks(t):
        ts = jnp.pad(to_sub(t), ((0, 0), (0, 0), (ATTN_BLOCK, Lp - L + ATTN_BLOCK), (0, 0), (0, 0)))
        ts = ts.reshape(B, dilation, nb + 2, ATTN_BLOCK, H, Dh)
        return jnp.concatenate([ts[:, :, :-2], ts[:, :, 1:-1], ts[:, :, 2:]], axis=3)

    kb = key_blocks(k)
    vb = key_blocks(v)

    scale = 1.0 / math.sqrt(Dh)
    scores = jnp.einsum('brnqhc,brnkhc->brnhqk', qb.astype(jnp.float32), kb.astype(jnp.float32)) * scale
    qi = jnp.arange(nb)[:, None] * ATTN_BLOCK + jnp.arange(ATTN_BLOCK)[None, :]
    kj = jnp.arange(nb)[:, None] * ATTN_BLOCK - ATTN_BLOCK + jnp.arange(3 * ATTN_BLOCK)[None, :]
    rel = kj[:, None, :] - qi[:, :, None]
    valid = (jnp.abs(rel) <= half) & (kj[:, None, :] >= 0) & (kj[:, None, :] < L)
    dist = (dilation * jnp.abs(rel)).astype(jnp.float32)
    bias = -slopes[None, :, None, None] * dist[:, None]
    scores = jnp.where(valid[:, None], scores + bias, NEG_INF)
    m = jnp.max(scores, axis=-1, keepdims=True)
    pexp = jnp.exp(scores - m)
    den = jnp.sum(pexp, axis=-1, keepdims=True)
    lse = (m + jnp.log(den))[..., 0]
    o = jnp.einsum('brnhqk,brnkhc->brnqhc', (pexp / den).astype(v.dtype), vb)
    o = o.reshape(B, dilation, Lp, H, Dh)[:, :, :L].transpose(0, 2, 1, 3, 4).reshape(B, S, H, Dh)
    lse = lse.transpose(0, 1, 2, 4, 3).reshape(B, dilation, Lp, H)[:, :, :L]
    lse = lse.transpose(0, 2, 1, 3).reshape(B, S, H)
    return o, lse


def peer_ffn(h, w_q, subkeys, u, v):
    B, S, D = h.shape
    n_chunks = (B * S) // PEER_CHUNK
    hc = h.reshape(n_chunks, PEER_CHUNK, D)

    def chunk(ht):
        q = (ht @ w_q).reshape(PEER_CHUNK, PEER_HEADS, 2, PEER_HALF)
        s = jnp.einsum('thpc,hpnc->thpn', q.astype(jnp.float32), subkeys.astype(jnp.float32))
        sv, si = lax.top_k(s, PEER_TOPK)
        cand_s = sv[:, :, 0, :, None] + sv[:, :, 1, None, :]
        cand_i = si[:, :, 0, :, None] * PEER_N_KEYS + si[:, :, 1, None, :]
        cand_s = cand_s.reshape(PEER_CHUNK, PEER_HEADS, PEER_TOPK * PEER_TOPK)
        cand_i = cand_i.reshape(PEER_CHUNK, PEER_HEADS, PEER_TOPK * PEER_TOPK)
        top_s, top_pos = lax.top_k(cand_s, PEER_TOPK)
        expert = jnp.take_along_axis(cand_i, top_pos, axis=-1)
        gate = jax.nn.softmax(top_s, axis=-1)
        ue = u[expert]
        ve = v[expert]
        act = jax.nn.gelu(jnp.einsum('thkd,td->thk', ue, ht).astype(jnp.float32), approximate=False)
        return jnp.einsum('thk,thkd->td', (gate * act).astype(ht.dtype), ve)

    return lax.map(chunk, hc).reshape(B, S, D)


def setup_inputs(seed: int = 0) -> dict:
    key = jax.random.key(seed)
    ks = jax.random.split(key, 12)
    f32 = jnp.float32
    x = jax.random.normal(ks[0], (BATCH, SEQ, D_MODEL), f32)
    g_mix = 1.0 + 0.02 * jax.random.normal(ks[1], (DEPTH, D_MODEL), f32)
    w_in = jax.random.normal(ks[2], (DEPTH, D_MODEL, IN_PROJ_WIDTH), f32) * D_MODEL ** -0.5
    w_pool = jax.random.normal(ks[3], (DEPTH, len(POOL_WINDOWS), POOL_GROUP, POOL_GROUP), f32) * POOL_GROUP ** -0.5
    pool_scale = 1.0 + 0.02 * jax.random.normal(ks[4], (DEPTH, POOL_WIDTH), f32)
    w_out = jax.random.normal(ks[5], (DEPTH, MIX_WIDTH, D_MODEL), f32) * MIX_WIDTH ** -0.5
    g_ffn = 1.0 + 0.02 * jax.random.normal(ks[6], (DEPTH, D_MODEL), f32)
    w_peer_q = jax.random.normal(ks[7], (DEPTH, D_MODEL, PEER_HEADS * PEER_QUERY_DIM), f32) * D_MODEL ** -0.5
    peer_subkeys = jax.random.normal(ks[8], (DEPTH, PEER_HEADS, 2, PEER_N_KEYS, PEER_HALF), f32) * PEER_HALF ** -0.5
    peer_u = jax.random.normal(ks[9], (DEPTH, PEER_N_EXPERTS, D_MODEL), f32) * D_MODEL ** -0.5
    peer_v = jax.random.normal(ks[10], (DEPTH, PEER_N_EXPERTS, D_MODEL), f32) * PEER_HEADS ** -0.5
    g_final = 1.0 + 0.02 * jax.random.normal(ks[11], (D_MODEL,), f32)
    return {"x": x, "g_mix": g_mix, "w_in": w_in, "w_pool": w_pool, "pool_scale": pool_scale,
            "w_out": w_out, "g_ffn": g_ffn, "w_peer_q": w_peer_q, "peer_subkeys": peer_subkeys,
            "peer_u": peer_u, "peer_v": peer_v, "g_final": g_final}


def reference(x, g_mix, w_in, w_pool, pool_scale, w_out, g_ffn, w_peer_q, peer_subkeys,
              peer_u, peer_v, g_final):
    B, S, D = x.shape
    slopes = alibi_slopes()
    for layer in range(DEPTH):
        h = rms_norm(x, g_mix[layer])
        proj = h @ w_in[layer]
        p = proj[..., :POOL_WIDTH]
        q = proj[..., POOL_WIDTH:POOL_WIDTH + ATTN_WIDTH].reshape(B, S, N_ATTN_HEADS, HEAD_DIM)
        k = proj[..., POOL_WIDTH + ATTN_WIDTH:POOL_WIDTH + 2 * ATTN_WIDTH].reshape(B, S, N_ATTN_HEADS, HEAD_DIM)
        v = proj[..., POOL_WIDTH + 2 * ATTN_WIDTH:].reshape(B, S, N_ATTN_HEADS, HEAD_DIM)

        pool_out = multiscale_pool(p, w_pool[layer], pool_scale[layer])

        outs, lses = [], []
        for gi, (window, dilation) in enumerate(ATTN_CONFIGS):
            hs = slice(gi * HEADS_PER_GROUP, (gi + 1) * HEADS_PER_GROUP)
            o, lse = dilated_window_attention(q[:, :, hs], k[:, :, hs], v[:, :, hs],
                                              window, dilation, slopes[hs])
            outs.append(o)
            lses.append(lse)
        alpha = jax.nn.softmax(jnp.stack(lses, axis=2), axis=2)
        attn = (jnp.stack(outs, axis=2).astype(jnp.float32) * alpha[..., None]).astype(x.dtype)
        attn = attn.reshape(B, S, ATTN_WIDTH)

        mixed = jnp.concatenate([pool_out, attn], axis=-1) @ w_out[layer]
        x = x + mixed

        x = x + peer_ffn(rms_norm(x, g_ffn[layer]), w_peer_q[layer], peer_subkeys[layer],
                         peer_u[layer], peer_v[layer])
    return rms_norm(x, g_final)
```

```python
import functools
import math

import numpy as np
import jax
import jax.numpy as jnp
from jax import lax
from jax.experimental import pallas as pl
from jax.experimental.pallas import tpu as pltpu

D_MODEL = 1024
LANES = 128
HEAD_DIM = 64
POOL_WINDOWS = (2, 4, 8, 16)
POOL_GROUP = 64
POOL_WIDTH = POOL_GROUP * len(POOL_WINDOWS)
ATTN_CONFIGS = ((128, 1), (512, 4), (2048, 16))
HEADS_PER_GROUP = 4
GROUP_WIDTH = HEADS_PER_GROUP * HEAD_DIM
N_ATTN_HEADS = HEADS_PER_GROUP * len(ATTN_CONFIGS)
ATTN_WIDTH = N_ATTN_HEADS * HEAD_DIM
IN_PROJ_WIDTH = POOL_WIDTH + 3 * ATTN_WIDTH
ATTN_HALF = 64
PEER_HEADS = 8
PEER_N_KEYS = 128
PEER_HALF = 128
PEER_TOPK = 16
PEER_SLOTS = PEER_HEADS * PEER_TOPK
RMS_EPS = 1e-6
NEG_INF = -1e30

VMEM_LIMIT = 48 * 1024 * 1024
POOL_PAD_LO = 16
POOL_PAD = 48
QUERY_BLOCK = 128
EXPERT_SUB = 16
EXPERT_TILE = 128

_NT = (((1,), (1,)), ((), ()))


def _alibi_slopes():
    return np.asarray(2.0 ** (-8.0 * np.arange(1, N_ATTN_HEADS + 1) / N_ATTN_HEADS), dtype=np.float32)


def _rms(x, g):
    return x * lax.rsqrt(jnp.mean(x * x, axis=-1, keepdims=True) + RMS_EPS) * g


def _gelu_exact(x):
    return 0.5 * x * (1.0 + lax.erf(x * math.sqrt(0.5)))


def _inproj_kernel(x_ref, g_ref, w_ref, o_ref):
    h = _rms(x_ref[...], g_ref[...])
    o_ref[...] = jnp.dot(h.astype(jnp.bfloat16), w_ref[...], preferred_element_type=jnp.float32)


def _inproj(x2, g, w_bf16, tm=512):
    T = x2.shape[0]
    return pl.pallas_call(
        _inproj_kernel,
        grid=(T // tm,),
        in_specs=[pl.BlockSpec((tm, D_MODEL), lambda i: (i, 0)),
                  pl.BlockSpec((1, D_MODEL), lambda i: (0, 0)),
                  pl.BlockSpec((D_MODEL, IN_PROJ_WIDTH), lambda i: (0, 0))],
        out_specs=pl.BlockSpec((tm, IN_PROJ_WIDTH), lambda i: (i, 0)),
        out_shape=jax.ShapeDtypeStruct((T, IN_PROJ_WIDTH), jnp.float32),
        compiler_params=pltpu.CompilerParams(dimension_semantics=("arbitrary",),
                                             vmem_limit_bytes=VMEM_LIMIT),
        name="inproj",
    )(x2, g, w_bf16)


def _pool_kernel(p_ref, w_ref, sc_ref, o_ref, a_ref, b_ref, c_ref):
    S = p_ref.shape[1]
    p = p_ref[0]
    lo = POOL_PAD_LO
    a_ref[pl.ds(0, lo), :] = jnp.zeros((lo, POOL_WIDTH), jnp.float32)
    a_ref[pl.ds(lo + S, POOL_PAD - lo), :] = jnp.zeros((POOL_PAD - lo, POOL_WIDTH), jnp.float32)
    a_ref[pl.ds(lo, S), :] = p
    n2, n4, n8 = S + 40, S + 32, S + 24
    b_ref[pl.ds(0, n2), :] = a_ref[pl.ds(0, n2), :] + a_ref[pl.ds(1, n2), :]
    c_ref[pl.ds(0, n4), :] = b_ref[pl.ds(0, n4), :] + b_ref[pl.ds(2, n4), :]
    w2 = b_ref[pl.ds(lo - 1, S), :]
    w4 = c_ref[pl.ds(lo - 2, S), :]
    b_ref[pl.ds(0, n8), :] = c_ref[pl.ds(0, n8), :] + c_ref[pl.ds(4, n8), :]
    w8 = b_ref[pl.ds(lo - 4, S), :]
    w16 = b_ref[pl.ds(lo - 8, S), :] + b_ref[pl.ds(lo, S), :]
    lane = lax.broadcasted_iota(jnp.int32, (S, POOL_WIDTH), 1)
    t = lax.broadcasted_iota(jnp.int32, (S, POOL_WIDTH), 0)
    g0, g1, g2 = lane < POOL_GROUP, lane < 2 * POOL_GROUP, lane < 3 * POOL_GROUP
    wsum = jnp.where(g0, w2, jnp.where(g1, w4, jnp.where(g2, w8, w16)))
    half = jnp.where(g0, 1, jnp.where(g1, 2, jnp.where(g2, 4, 8)))
    cnt = jnp.minimum(t + half, S) - jnp.maximum(t - half, 0)
    pooled = wsum / cnt.astype(jnp.float32) - p
    mixed = jnp.dot(pooled.astype(jnp.bfloat16), w_ref[...], preferred_element_type=jnp.float32)
    o_ref[0] = mixed * sc_ref[...]


def _pool(proj3, w_blockdiag, scale):
    B, S, _ = proj3.shape
    return pl.pallas_call(
        _pool_kernel,
        grid=(B,),
        in_specs=[pl.BlockSpec((1, S, POOL_WIDTH), lambda b: (b, 0, 0)),
                  pl.BlockSpec((POOL_WIDTH, POOL_WIDTH), lambda b: (0, 0)),
                  pl.BlockSpec((1, POOL_WIDTH), lambda b: (0, 0))],
        out_specs=pl.BlockSpec((1, S, POOL_WIDTH), lambda b: (b, 0, 0)),
        out_shape=jax.ShapeDtypeStruct((B, S, POOL_WIDTH), jnp.float32),
        scratch_shapes=[pltpu.VMEM((S + POOL_PAD, POOL_WIDTH), jnp.float32)] * 3,
        compiler_params=pltpu.CompilerParams(dimension_semantics=("arbitrary",),
                                             vmem_limit_bytes=VMEM_LIMIT),
        name="pool",
    )(proj3, w_blockdiag, scale)


def _attn_kernel(q0, q1, k0, k1, v0, v1, o0, o1, l0, l1, qs, ks, vs, os_, ls, *, dilation, slopes):
    S = q0.shape[1]
    d = dilation
    L = S // d
    QB = min(QUERY_BLOCK, L)
    W = min(L, QB + 2 * ATTN_HALF)
    nb = L // QB
    for r in range(d):
        rows = pl.ds(r, L, stride=d) if d > 1 else pl.ds(0, L)
        for dst, halves in ((qs, (q0, q1)), (ks, (k0, k1)), (vs, (v0, v1))):
            for c, src in enumerate(halves):
                dst[r, :, pl.ds(c * LANES, LANES)] = src[0, rows, :].astype(jnp.bfloat16)

    head_of_lane = lax.broadcasted_iota(jnp.int32, (QB, GROUP_WIDTH), 1) // HEAD_DIM
    col_minus_row = (lax.broadcasted_iota(jnp.int32, (QB, W), 1)
                     - lax.broadcasted_iota(jnp.int32, (QB, W), 0))
    scale = 1.0 / math.sqrt(HEAD_DIM)

    def block(n, carry):
        r = n // nb
        i0 = pl.multiple_of((n % nb) * QB, QB)
        ws = pl.multiple_of(jnp.clip(i0 - ATTN_HALF, 0, L - W), ATTN_HALF)
        q = qs[r, pl.ds(i0, QB), :]
        k = ks[r, pl.ds(ws, W), :]
        v = vs[r, pl.ds(ws, W), :]
        arel = jnp.abs(col_minus_row + (ws - i0))
        valid = arel <= ATTN_HALF
        dist = (d * arel).astype(jnp.float32)
        o_acc = jnp.zeros((QB, GROUP_WIDTH), jnp.float32)
        l_acc = jnp.zeros((QB, GROUP_WIDTH), jnp.float32)
        for h in range(HEADS_PER_GROUP):
            mine = head_of_lane == h
            qh = jnp.where(mine, q, jnp.zeros_like(q))
            s = lax.dot_general(qh, k, _NT, preferred_element_type=jnp.float32) * scale
            s = jnp.where(valid, s - slopes[h] * dist, NEG_INF)
            m = jnp.max(s, axis=-1, keepdims=True)
            pexp = jnp.exp(s - m)
            den = jnp.sum(pexp, axis=-1, keepdims=True)
            lse = m + jnp.log(den)
            oh = jnp.dot(pexp.astype(jnp.bfloat16), v, preferred_element_type=jnp.float32) / den
            o_acc = jnp.where(mine, oh, o_acc)
            l_acc = jnp.where(mine, lse, l_acc)
        os_[r, pl.ds(i0, QB), :] = o_acc
        ls[r, pl.ds(i0, QB), :] = l_acc
        return carry

    lax.fori_loop(0, d * nb, block, 0)
    for r in range(d):
        rows = pl.ds(r, L, stride=d) if d > 1 else pl.ds(0, L)
        for c, (o_ref, l_ref) in enumerate(((o0, l0), (o1, l1))):
            o_ref[0, rows, :] = os_[r, :, pl.ds(c * LANES, LANES)]
            l_ref[0, rows, :] = ls[r, :, pl.ds(c * LANES, LANES)]


def _attention(proj3, group):
    B, S, _ = proj3.shape
    _, d = ATTN_CONFIGS[group]
    L = S // d
    slopes = tuple(float(s) for s in _alibi_slopes()[group * HEADS_PER_GROUP:(group + 1) * HEADS_PER_GROUP])
    per_group = GROUP_WIDTH // LANES
    first = POOL_WIDTH // LANES + group * per_group
    step = len(ATTN_CONFIGS) * per_group
    spec = lambda cb: pl.BlockSpec((1, S, LANES), lambda b: (b, 0, cb))
    out_spec = pl.BlockSpec((1, S, LANES), lambda b: (b, 0, 0))
    return pl.pallas_call(
        functools.partial(_attn_kernel, dilation=d, slopes=slopes),
        grid=(B,),
        in_specs=[spec(first + t * step + c) for t in range(3) for c in range(per_group)],
        out_specs=[out_spec] * 4,
        out_shape=[jax.ShapeDtypeStruct((B, S, LANES), jnp.float32)] * 4,
        scratch_shapes=[pltpu.VMEM((d, L, GROUP_WIDTH), jnp.bfloat16)] * 3
                       + [pltpu.VMEM((d, L, GROUP_WIDTH), jnp.float32)] * 2,
        compiler_params=pltpu.CompilerParams(dimension_semantics=("arbitrary",),
                                             vmem_limit_bytes=VMEM_LIMIT),
        name=f"attn_d{d}",
    )(*([proj3] * (3 * per_group)))


def _outproj_kernel(x_ref, pool_ref, *refs):
    ng = len(ATTN_CONFIGS)
    per_group = GROUP_WIDTH // LANES
    o_refs, l_refs = refs[:ng * per_group], refs[ng * per_group:2 * ng * per_group]
    w_ref, g_ref, x1_ref, h_ref = refs[2 * ng * per_group:]
    acc = x_ref[...] + jnp.dot(pool_ref[...].astype(jnp.bfloat16), w_ref[pl.ds(0, POOL_WIDTH), :],
                               preferred_element_type=jnp.float32)
    for c in range(per_group):
        lse = [l_refs[g * per_group + c][...] for g in range(ng)]
        m = functools.reduce(jnp.maximum, lse)
        e = [jnp.exp(l - m) for l in lse]
        tot = functools.reduce(jnp.add, e)
        for g in range(ng):
            part = o_refs[g * per_group + c][...] * (e[g] / tot)
            row0 = POOL_WIDTH + g * GROUP_WIDTH + c * LANES
            acc = acc + jnp.dot(part.astype(jnp.bfloat16), w_ref[pl.ds(row0, LANES), :],
                                preferred_element_type=jnp.float32)
    x1_ref[...] = acc
    h_ref[...] = _rms(acc, g_ref[...]).astype(jnp.bfloat16)


def _outproj(x2, pool2, os_, ls_, w_bf16, g, tm=512):
    T = x2.shape[0]
    wide = pl.BlockSpec((tm, D_MODEL), lambda i: (i, 0))
    return pl.pallas_call(
        _outproj_kernel,
        grid=(T // tm,),
        in_specs=[wide, pl.BlockSpec((tm, POOL_WIDTH), lambda i: (i, 0))]
                 + [pl.BlockSpec((tm, LANES), lambda i: (i, 0))] * (len(os_) + len(ls_))
                 + [pl.BlockSpec((D_MODEL, D_MODEL), lambda i: (0, 0)),
                    pl.BlockSpec((1, D_MODEL), lambda i: (0, 0))],
        out_specs=[wide, wide],
        out_shape=[jax.ShapeDtypeStruct((T, D_MODEL), jnp.float32),
                   jax.ShapeDtypeStruct((T, D_MODEL), jnp.bfloat16)],
        compiler_params=pltpu.CompilerParams(dimension_semantics=("arbitrary",),
                                             vmem_limit_bytes=VMEM_LIMIT),
        name="outproj",
    )(x2, pool2, *os_, *ls_, w_bf16, g)


def _extract_topk(s, payload, k):
    n = s.shape[0]
    row = lax.broadcasted_iota(jnp.int32, s.shape, 0)
    vals, outs = [], []
    for _ in range(k):
        m = jnp.max(s, axis=0, keepdims=True)
        pos = jnp.min(jnp.where(s == m, row, n), axis=0, keepdims=True)
        hit = row == pos
        vals.append(m)
        outs.append(pos if payload is None else jnp.max(jnp.where(hit, payload, -1), axis=0, keepdims=True))
        s = jnp.where(hit, -jnp.inf, s)
    return jnp.concatenate(vals, axis=0), jnp.concatenate(outs, axis=0)


def _route_kernel(h_ref, wq_ref, sk_ref, idx_ref, gate_ref, q_scr, sv_scr, si_scr, e_scr, g_scr):
    tm = h_ref.shape[0]
    q = jnp.dot(h_ref[...], wq_ref[...], preferred_element_type=jnp.float32).astype(jnp.bfloat16)
    for c in range(2 * PEER_HEADS):
        q_scr[c] = q[:, c * PEER_HALF:(c + 1) * PEER_HALF]

    def subkeys(c, carry):
        s = lax.dot_general(sk_ref[c], q_scr[c], _NT, preferred_element_type=jnp.float32)
        sv, si = _extract_topk(s, None, PEER_TOPK)
        sv_scr[c] = sv
        si_scr[c] = si
        return carry

    lax.fori_loop(0, 2 * PEER_HEADS, subkeys, 0)

    def head(hd, carry):
        sv0, sv1 = sv_scr[2 * hd], sv_scr[2 * hd + 1]
        si0, si1 = si_scr[2 * hd], si_scr[2 * hd + 1]
        cand_s = jnp.concatenate([sv0[a:a + 1, :] + sv1 for a in range(PEER_TOPK)], axis=0)
        cand_i = jnp.concatenate([si0[a:a + 1, :] * PEER_N_KEYS + si1 for a in range(PEER_TOPK)], axis=0)
        top_s, expert = _extract_topk(cand_s, cand_i, PEER_TOPK)
        e = jnp.exp(top_s - top_s[0:1, :])
        g_scr[hd] = e / jnp.sum(e, axis=0, keepdims=True)
        e_scr[hd] = expert
        return carry

    lax.fori_loop(0, PEER_HEADS, head, 0)
    idx_ref[...] = e_scr[...].reshape(PEER_SLOTS, tm).T
    gate_ref[...] = g_scr[...].reshape(PEER_SLOTS, tm).T


def _route(h2, wq_bf16, sk_bf16, tm=128):
    T = h2.shape[0]
    nq = 2 * PEER_HEADS
    return pl.pallas_call(
        _route_kernel,
        grid=(T // tm,),
        in_specs=[pl.BlockSpec((tm, D_MODEL), lambda i: (i, 0)),
                  pl.BlockSpec((D_MODEL, nq * PEER_HALF), lambda i: (0, 0)),
                  pl.BlockSpec((nq, PEER_N_KEYS, PEER_HALF), lambda i: (0, 0, 0))],
        out_specs=[pl.BlockSpec((tm, PEER_SLOTS), lambda i: (i, 0))] * 2,
        out_shape=[jax.ShapeDtypeStruct((T, PEER_SLOTS), jnp.int32),
                   jax.ShapeDtypeStruct((T, PEER_SLOTS), jnp.float32)],
        scratch_shapes=[pltpu.VMEM((nq, tm, PEER_HALF), jnp.bfloat16),
                        pltpu.VMEM((nq, PEER_TOPK, tm), jnp.float32),
                        pltpu.VMEM((nq, PEER_TOPK, tm), jnp.int32),
                        pltpu.VMEM((PEER_HEADS, PEER_TOPK, tm), jnp.int32),
                        pltpu.VMEM((PEER_HEADS, PEER_TOPK, tm), jnp.float32)],
        compiler_params=pltpu.CompilerParams(dimension_semantics=("arbitrary",),
                                             vmem_limit_bytes=VMEM_LIMIT),
        name="route",
    )(h2, wq_bf16, sk_bf16)


def _expert_kernel(idx_ref, gate_ref, h_ref, x1_ref, gf_ref, u_hbm, v_hbm, o_ref, ubuf, vbuf, sem):
    i = pl.program_id(0)
    n = pl.num_programs(0)
    nsub = EXPERT_TILE // EXPERT_SUB
    rows = EXPERT_SUB * PEER_SLOTS

    def issue(sub, slot):
        def per_head(m, carry):
            t = m // PEER_HEADS
            j0 = (m % PEER_HEADS) * PEER_TOPK
            for j in range(PEER_TOPK):
                e = idx_ref[sub * EXPERT_SUB + t, j0 + j]
                dst = pl.ds(t * PEER_SLOTS + j0 + j, 1)
                pltpu.make_async_copy(u_hbm.at[pl.ds(e, 1), :], ubuf.at[slot, dst, :], sem.at[0, slot]).start()
                pltpu.make_async_copy(v_hbm.at[pl.ds(e, 1), :], vbuf.at[slot, dst, :], sem.at[1, slot]).start()
            return carry
        lax.fori_loop(0, EXPERT_SUB * PEER_HEADS, per_head, 0)

    def wait(slot):
        pltpu.make_async_copy(u_hbm.at[pl.ds(0, rows), :], ubuf.at[slot], sem.at[0, slot]).wait()
        pltpu.make_async_copy(v_hbm.at[pl.ds(0, rows), :], vbuf.at[slot], sem.at[1, slot]).wait()

    @pl.when(i == 0)
    def _():
        issue(0, 0)

    row_id = lax.broadcasted_iota(jnp.int32, (EXPERT_SUB, PEER_SLOTS), 0)
    row_wide = lax.broadcasted_iota(jnp.int32, (EXPERT_SUB, rows), 0)
    blk_wide = lax.broadcasted_iota(jnp.int32, (EXPERT_SUB, rows), 1) // PEER_SLOTS

    def sub_tile(s, carry):
        slot = s % 2

        @pl.when(jnp.logical_or(s + 1 < nsub, i + 1 < n))
        def _():
            issue(s + 1, 1 - slot)

        wait(slot)
        tok = pl.ds(pl.multiple_of(s * EXPERT_SUB, EXPERT_SUB), EXPERT_SUB)
        h = h_ref[tok, :]
        a = lax.dot_general(h, ubuf[slot].astype(jnp.bfloat16), _NT, preferred_element_type=jnp.float32)
        act = jnp.zeros((EXPERT_SUB, PEER_SLOTS), jnp.float32)
        for c in range(EXPERT_SUB):
            act = act + jnp.where(row_id == c, a[:, c * PEER_SLOTS:(c + 1) * PEER_SLOTS], 0.0)
        wgt = gate_ref[tok, :] * _gelu_exact(act)
        wide = jnp.where(blk_wide == row_wide, jnp.concatenate([wgt] * EXPERT_SUB, axis=1), 0.0)
        y = jnp.dot(wide.astype(jnp.bfloat16), vbuf[slot].astype(jnp.bfloat16),
                    preferred_element_type=jnp.float32)
        o_ref[tok, :] = _rms(x1_ref[tok, :] + y, gf_ref[...])
        return carry

    lax.fori_loop(0, nsub, sub_tile, 0)


def _experts(idx, gate, h2, x1, g_final, u, v):
    T = idx.shape[0]
    n = T // EXPERT_TILE
    rows = EXPERT_SUB * PEER_SLOTS
    tile = lambda w: pl.BlockSpec((EXPERT_TILE, w), lambda i: (i, 0))
    idx3 = idx.reshape(n, EXPERT_TILE, PEER_SLOTS)
    idx_ext = jnp.concatenate([idx3, jnp.roll(idx3[:, :EXPERT_SUB], -1, axis=0)], axis=1)
    return pl.pallas_call(
        _expert_kernel,
        grid=(n,),
        in_specs=[pl.BlockSpec((None, EXPERT_TILE + EXPERT_SUB, PEER_SLOTS), lambda i: (i, 0, 0),
                               memory_space=pltpu.SMEM),
                  tile(PEER_SLOTS), tile(D_MODEL), tile(D_MODEL),
                  pl.BlockSpec((1, D_MODEL), lambda i: (0, 0)),
                  pl.BlockSpec(memory_space=pl.ANY),
                  pl.BlockSpec(memory_space=pl.ANY)],
        out_specs=tile(D_MODEL),
        out_shape=jax.ShapeDtypeStruct((T, D_MODEL), jnp.float32),
        scratch_shapes=[pltpu.VMEM((2, rows, D_MODEL), jnp.float32),
                        pltpu.VMEM((2, rows, D_MODEL), jnp.float32),
                        pltpu.SemaphoreType.DMA((2, 2))],
        compiler_params=pltpu.CompilerParams(dimension_semantics=("arbitrary",),
                                             vmem_limit_bytes=VMEM_LIMIT),
        name="experts",
    )(idx_ext, gate, h2, x1, g_final, u, v)


def _layer(x, g_mix, w_in, w_pool, pool_scale, w_out, g_ffn, w_peer_q, peer_subkeys, peer_u, peer_v,
           g_out):
    B, S, D = x.shape
    T = B * S
    bf = jnp.bfloat16
    x2 = x.reshape(T, D)
    proj3 = _inproj(x2, g_mix.reshape(1, D), w_in.astype(bf)).reshape(B, S, IN_PROJ_WIDTH)

    w_bd = jnp.zeros((POOL_WIDTH, POOL_WIDTH), jnp.float32)
    for gi in range(len(POOL_WINDOWS)):
        sl = slice(gi * POOL_GROUP, (gi + 1) * POOL_GROUP)
        w_bd = w_bd.at[sl, sl].set(w_pool[gi])
    pool2 = _pool(proj3, w_bd.astype(bf), pool_scale.reshape(1, POOL_WIDTH)).reshape(T, POOL_WIDTH)

    outs, lses = [], []
    for gi in range(len(ATTN_CONFIGS)):
        o_lo, o_hi, l_lo, l_hi = _attention(proj3, gi)
        outs += [o_lo.reshape(T, LANES), o_hi.reshape(T, LANES)]
        lses += [l_lo.reshape(T, LANES), l_hi.reshape(T, LANES)]

    x1, h2 = _outproj(x2, pool2, outs, lses, w_out.astype(bf), g_ffn.reshape(1, D))
    sk = peer_subkeys.reshape(2 * PEER_HEADS, PEER_N_KEYS, PEER_HALF).astype(bf)
    idx, gate = _route(h2, w_peer_q.astype(bf), sk)
    y = _experts(idx, gate, h2, x1, g_out.reshape(1, D), peer_u, peer_v)
    return y.reshape(B, S, D)


def kernel(x, g_mix, w_in, w_pool, pool_scale, w_out, g_ffn, w_peer_q, peer_subkeys, peer_u, peer_v,
           g_final):
    assert g_mix.shape[0] == 1, "single-layer block"
    return _layer(x, g_mix[0], w_in[0], w_pool[0], pool_scale[0], w_out[0], g_ffn[0], w_peer_q[0],
                  peer_subkeys[0], peer_u[0], peer_v[0], g_final)
```

```python
import functools
import math

import numpy as np
import jax
import jax.numpy as jnp
from jax import lax
from jax.experimental import pallas as pl
from jax.experimental.pallas import tpu as pltpu

D_MODEL = 1024
LANES = 128
SUBLANES = 8
HEAD_DIM = 64
POOL_WINDOWS = (2, 4, 8, 16)
POOL_GROUP = 64
POOL_WIDTH = POOL_GROUP * len(POOL_WINDOWS)
ATTN_CONFIGS = ((128, 1), (512, 4), (2048, 16))
HEADS_PER_GROUP = 4
GROUP_WIDTH = HEADS_PER_GROUP * HEAD_DIM
N_ATTN_HEADS = HEADS_PER_GROUP * len(ATTN_CONFIGS)
ATTN_WIDTH = N_ATTN_HEADS * HEAD_DIM
IN_PROJ_WIDTH = POOL_WIDTH + 3 * ATTN_WIDTH
ATTN_HALF = 64
PEER_HEADS = 8
PEER_N_KEYS = 128
PEER_HALF = 128
PEER_TOPK = 16
PEER_SLOTS = PEER_HEADS * PEER_TOPK
RMS_EPS = 1e-6
NEG_INF = -1e30

VMEM_LIMIT = 48 * 1024 * 1024
POOL_PAD_LO = 16
POOL_PAD = 48
QUERY_BLOCK = 128
EXPERT_SUB = 16
EXPERT_TILE = 128

_NT = (((1,), (1,)), ((), ()))


def _alibi_slopes():
    return np.asarray(2.0 ** (-8.0 * np.arange(1, N_ATTN_HEADS + 1) / N_ATTN_HEADS), dtype=np.float32)


def _rms(x, g):
    return x * lax.rsqrt(jnp.mean(x * x, axis=-1, keepdims=True) + RMS_EPS) * g


def _gelu_exact(x):
    return 0.5 * x * (1.0 + lax.erf(x * math.sqrt(0.5)))


def _inproj_kernel(x_ref, g_ref, w_ref, o_ref):
    h = _rms(x_ref[...], g_ref[...])
    o_ref[...] = jnp.dot(h.astype(jnp.bfloat16), w_ref[...], preferred_element_type=jnp.float32)


def _inproj(x2, g, w_bf16, tm=512):
    T = x2.shape[0]
    return pl.pallas_call(
        _inproj_kernel,
        grid=(T // tm,),
        in_specs=[pl.BlockSpec((tm, D_MODEL), lambda i: (i, 0)),
                  pl.BlockSpec((1, D_MODEL), lambda i: (0, 0)),
                  pl.BlockSpec((D_MODEL, IN_PROJ_WIDTH), lambda i: (0, 0))],
        out_specs=pl.BlockSpec((tm, IN_PROJ_WIDTH), lambda i: (i, 0)),
        out_shape=jax.ShapeDtypeStruct((T, IN_PROJ_WIDTH), jnp.float32),
        compiler_params=pltpu.CompilerParams(dimension_semantics=("arbitrary",),
                                             vmem_limit_bytes=VMEM_LIMIT),
        name="inproj",
    )(x2, g, w_bf16)


def _pool_kernel(p_ref, w_ref, sc_ref, o_ref, a_ref, b_ref, c_ref):
    S = p_ref.shape[1]
    p = p_ref[0]
    lo = POOL_PAD_LO
    a_ref[pl.ds(0, lo), :] = jnp.zeros((lo, POOL_WIDTH), jnp.float32)
    a_ref[pl.ds(lo + S, POOL_PAD - lo), :] = jnp.zeros((POOL_PAD - lo, POOL_WIDTH), jnp.float32)
    a_ref[pl.ds(lo, S), :] = p
    n2, n4, n8 = S + 40, S + 32, S + 24
    b_ref[pl.ds(0, n2), :] = a_ref[pl.ds(0, n2), :] + a_ref[pl.ds(1, n2), :]
    c_ref[pl.ds(0, n4), :] = b_ref[pl.ds(0, n4), :] + b_ref[pl.ds(2, n4), :]
    w2 = b_ref[pl.ds(lo - 1, S), :]
    w4 = c_ref[pl.ds(lo - 2, S), :]
    b_ref[pl.ds(0, n8), :] = c_ref[pl.ds(0, n8), :] + c_ref[pl.ds(4, n8), :]
    w8 = b_ref[pl.ds(lo - 4, S), :]
    w16 = b_ref[pl.ds(lo - 8, S), :] + b_ref[pl.ds(lo, S), :]
    lane = lax.broadcasted_iota(jnp.int32, (S, POOL_WIDTH), 1)
    t = lax.broadcasted_iota(jnp.int32, (S, POOL_WIDTH), 0)
    g0, g1, g2 = lane < POOL_GROUP, lane < 2 * POOL_GROUP, lane < 3 * POOL_GROUP
    wsum = jnp.where(g0, w2, jnp.where(g1, w4, jnp.where(g2, w8, w16)))
    half = jnp.where(g0, 1, jnp.where(g1, 2, jnp.where(g2, 4, 8)))
    cnt = jnp.minimum(t + half, S) - jnp.maximum(t - half, 0)
    pooled = wsum / cnt.astype(jnp.float32) - p
    mixed = jnp.dot(pooled.astype(jnp.bfloat16), w_ref[...], preferred_element_type=jnp.float32)
    o_ref[0] = mixed * sc_ref[...]


def _pool(proj3, w_blockdiag, scale):
    B, S, _ = proj3.shape
    return pl.pallas_call(
        _pool_kernel,
        grid=(B,),
        in_specs=[pl.BlockSpec((1, S, POOL_WIDTH), lambda b: (b, 0, 0)),
                  pl.BlockSpec((POOL_WIDTH, POOL_WIDTH), lambda b: (0, 0)),
                  pl.BlockSpec((1, POOL_WIDTH), lambda b: (0, 0))],
        out_specs=pl.BlockSpec((1, S, POOL_WIDTH), lambda b: (b, 0, 0)),
        out_shape=jax.ShapeDtypeStruct((B, S, POOL_WIDTH), jnp.float32),
        scratch_shapes=[pltpu.VMEM((S + POOL_PAD, POOL_WIDTH), jnp.float32)] * 3,
        compiler_params=pltpu.CompilerParams(dimension_semantics=("arbitrary",),
                                             vmem_limit_bytes=VMEM_LIMIT),
        name="pool",
    )(proj3, w_blockdiag, scale)


def _attn_kernel(q0, q1, k0, k1, v0, v1, o0, o1, l0, l1, qs, ks, vs, os_, ls, *, dilation, slopes):
    S = q0.shape[1]
    d = dilation
    L = S // d
    QB = min(QUERY_BLOCK, L)
    W = min(L, QB + 2 * ATTN_HALF)
    nb = L // QB
    for r in range(d):
        rows = pl.ds(r, L, stride=d) if d > 1 else pl.ds(0, L)
        for dst, halves in ((qs, (q0, q1)), (ks, (k0, k1)), (vs, (v0, v1))):
            for c, src in enumerate(halves):
                dst[r, :, pl.ds(c * LANES, LANES)] = src[0, rows, :].astype(jnp.bfloat16)

    head_of_lane = lax.broadcasted_iota(jnp.int32, (QB, GROUP_WIDTH), 1) // HEAD_DIM
    col_minus_row = (lax.broadcasted_iota(jnp.int32, (QB, W), 1)
                     - lax.broadcasted_iota(jnp.int32, (QB, W), 0))
    scale = 1.0 / math.sqrt(HEAD_DIM)

    def block(n, carry):
        r = n // nb
        i0 = pl.multiple_of((n % nb) * QB, QB)
        ws = pl.multiple_of(jnp.clip(i0 - ATTN_HALF, 0, L - W), ATTN_HALF)
        q = qs[r, pl.ds(i0, QB), :]
        k = ks[r, pl.ds(ws, W), :]
        v = vs[r, pl.ds(ws, W), :]
        arel = jnp.abs(col_minus_row + (ws - i0))
        valid = arel <= ATTN_HALF
        dist = (d * arel).astype(jnp.float32)
        o_acc = jnp.zeros((QB, GROUP_WIDTH), jnp.float32)
        l_acc = jnp.zeros((QB, GROUP_WIDTH), jnp.float32)
        for h in range(HEADS_PER_GROUP):
            mine = head_of_lane == h
            qh = jnp.where(mine, q, jnp.zeros_like(q))
            s = lax.dot_general(qh, k, _NT, preferred_element_type=jnp.float32) * scale
            s = jnp.where(valid, s - slopes[h] * dist, NEG_INF)
            m = jnp.max(s, axis=-1, keepdims=True)
            pexp = jnp.exp(s - m)
            den = jnp.sum(pexp, axis=-1, keepdims=True)
            lse = m + jnp.log(den)
            oh = jnp.dot(pexp.astype(jnp.bfloat16), v, preferred_element_type=jnp.float32) / den
            o_acc = jnp.where(mine, oh, o_acc)
            l_acc = jnp.where(mine, lse, l_acc)
        os_[r, pl.ds(i0, QB), :] = o_acc
        ls[r, pl.ds(i0, QB), :] = l_acc
        return carry

    lax.fori_loop(0, d * nb, block, 0)
    for r in range(d):
        rows = pl.ds(r, L, stride=d) if d > 1 else pl.ds(0, L)
        for c, (o_ref, l_ref) in enumerate(((o0, l0), (o1, l1))):
            o_ref[0, rows, :] = os_[r, :, pl.ds(c * LANES, LANES)]
            l_ref[0, rows, :] = ls[r, :, pl.ds(c * LANES, LANES)]


def _attention(proj3, group):
    B, S, _ = proj3.shape
    _, d = ATTN_CONFIGS[group]
    L = S // d
    slopes = tuple(float(s) for s in _alibi_slopes()[group * HEADS_PER_GROUP:(group + 1) * HEADS_PER_GROUP])
    per_group = GROUP_WIDTH // LANES
    first = POOL_WIDTH // LANES + group * per_group
    step = len(ATTN_CONFIGS) * per_group
    spec = lambda cb: pl.BlockSpec((1, S, LANES), lambda b: (b, 0, cb))
    out_spec = pl.BlockSpec((1, S, LANES), lambda b: (b, 0, 0))
    return pl.pallas_call(
        functools.partial(_attn_kernel, dilation=d, slopes=slopes),
        grid=(B,),
        in_specs=[spec(first + t * step + c) for t in range(3) for c in range(per_group)],
        out_specs=[out_spec] * 4,
        out_shape=[jax.ShapeDtypeStruct((B, S, LANES), jnp.float32)] * 4,
        scratch_shapes=[pltpu.VMEM((d, L, GROUP_WIDTH), jnp.bfloat16)] * 3
                       + [pltpu.VMEM((d, L, GROUP_WIDTH), jnp.float32)] * 2,
        compiler_params=pltpu.CompilerParams(dimension_semantics=("arbitrary",),
                                             vmem_limit_bytes=VMEM_LIMIT),
        name=f"attn_d{d}",
    )(*([proj3] * (3 * per_group)))


def _outproj_kernel(x_ref, pool_ref, *refs):
    ng = len(ATTN_CONFIGS)
    per_group = GROUP_WIDTH // LANES
    o_refs, l_refs = refs[:ng * per_group], refs[ng * per_group:2 * ng * per_group]
    w_ref, g_ref, x1_ref, h_ref = refs[2 * ng * per_group:]
    acc = x_ref[...] + jnp.dot(pool_ref[...].astype(jnp.bfloat16), w_ref[pl.ds(0, POOL_WIDTH), :],
                               preferred_element_type=jnp.float32)
    for c in range(per_group):
        lse = [l_refs[g * per_group + c][...] for g in range(ng)]
        m = functools.reduce(jnp.maximum, lse)
        e = [jnp.exp(l - m) for l in lse]
        tot = functools.reduce(jnp.add, e)
        for g in range(ng):
            part = o_refs[g * per_group + c][...] * (e[g] / tot)
            row0 = POOL_WIDTH + g * GROUP_WIDTH + c * LANES
            acc = acc + jnp.dot(part.astype(jnp.bfloat16), w_ref[pl.ds(row0, LANES), :],
                                preferred_element_type=jnp.float32)
    x1_ref[...] = acc
    h_ref[...] = _rms(acc, g_ref[...]).astype(jnp.bfloat16)


def _outproj(x2, pool2, os_, ls_, w_bf16, g, tm=512):
    T = x2.shape[0]
    wide = pl.BlockSpec((tm, D_MODEL), lambda i: (i, 0))
    return pl.pallas_call(
        _outproj_kernel,
        grid=(T // tm,),
        in_specs=[wide, pl.BlockSpec((tm, POOL_WIDTH), lambda i: (i, 0))]
                 + [pl.BlockSpec((tm, LANES), lambda i: (i, 0))] * (len(os_) + len(ls_))
                 + [pl.BlockSpec((D_MODEL, D_MODEL), lambda i: (0, 0)),
                    pl.BlockSpec((1, D_MODEL), lambda i: (0, 0))],
        out_specs=[wide, wide],
        out_shape=[jax.ShapeDtypeStruct((T, D_MODEL), jnp.float32),
                   jax.ShapeDtypeStruct((T, D_MODEL), jnp.bfloat16)],
        compiler_params=pltpu.CompilerParams(dimension_semantics=("arbitrary",),
                                             vmem_limit_bytes=VMEM_LIMIT),
        name="outproj",
    )(x2, pool2, *os_, *ls_, w_bf16, g)


_CANDIDATES = tuple((a, b) for a in range(PEER_TOPK) for b in range(PEER_TOPK)
                    if (a + 1) * (b + 1) <= PEER_TOPK)


def _reduce(op, xs, ways=4):
    acc = []
    for i, x in enumerate(xs):
        if i < ways:
            acc.append(x)
        else:
            acc[i % ways] = op(acc[i % ways], x)
    while len(acc) > 1:
        acc = [op(acc[i], acc[i + 1]) if i + 1 < len(acc) else acc[i] for i in range(0, len(acc), 2)]
    return acc[0]


def _topk_rows(row, n, k, emit):
    def removed(pos):
        for r in range(n):
            v = jnp.where(pos == r, -jnp.inf, row(r))
            row(r, v)
            yield v

    def body(i, m):
        pos = _reduce(jnp.minimum, (jnp.where(row(r) == m, r, n) for r in range(n)))
        emit(i, m, pos)
        return _reduce(jnp.maximum, removed(pos))

    lax.fori_loop(0, k, body, _reduce(jnp.maximum, (row(r) for r in range(n))))


def _route_kernel(h_ref, wq_ref, sk_ref, idx_ref, gate_ref, s0_scr, s1_scr, sv_scr, si_scr, cs_scr, ci_scr,
                  ts_scr, ex_scr):
    tm = h_ref.shape[0]
    q = jnp.dot(h_ref[...], wq_ref[...], preferred_element_type=jnp.float32).astype(jnp.bfloat16)
    s_scr = (s0_scr, s1_scr)
    for hd in range(PEER_HEADS):
        for p in range(2):
            c = 2 * hd + p
            s = lax.dot_general(sk_ref[c], q[:, c * PEER_HALF:(c + 1) * PEER_HALF], _NT,
                                preferred_element_type=jnp.float32)
            s_scr[p][pl.ds(hd, PEER_N_KEYS, stride=PEER_HEADS), :] = s

    for p in range(2):
        def row(r, new=None, p=p):
            if new is None:
                return s_scr[p][pl.ds(r * PEER_HEADS, PEER_HEADS), :]
            s_scr[p][pl.ds(r * PEER_HEADS, PEER_HEADS), :] = new

        def emit(i, value, r, p=p):
            sv_scr[p, i] = value
            si_scr[p, i] = r

        _topk_rows(row, PEER_N_KEYS, PEER_TOPK, emit)

    for r, (a, b) in enumerate(_CANDIDATES):
        cs_scr[r] = sv_scr[0, a] + sv_scr[1, b]
        ci_scr[r] = si_scr[0, a] * PEER_N_KEYS + si_scr[1, b]

    def cand(r, new=None):
        if new is None:
            return cs_scr[r]
        cs_scr[r] = new

    def emit_expert(i, value, r):
        ts_scr[i] = value
        ex_scr[i] = _reduce(jnp.maximum, (jnp.where(r == c, ci_scr[c], -1) for c in range(len(_CANDIDATES))))

    _topk_rows(cand, len(_CANDIDATES), PEER_TOPK, emit_expert)

    top_s = ts_scr[...]
    e = jnp.exp(top_s - top_s[0:1])
    gate = e / jnp.sum(e, axis=0, keepdims=True)
    idx_ref[...] = ex_scr[...].reshape(PEER_SLOTS, tm).T
    gate_ref[...] = gate.reshape(PEER_SLOTS, tm).T


def _route(h2, wq_bf16, sk_bf16):
    T = h2.shape[0]
    tm = LANES
    nq = 2 * PEER_HEADS
    tile = (PEER_HEADS, tm)
    assert PEER_HEADS == SUBLANES
    return pl.pallas_call(
        _route_kernel,
        grid=(T // tm,),
        in_specs=[pl.BlockSpec((tm, D_MODEL), lambda i: (i, 0)),
                  pl.BlockSpec((D_MODEL, nq * PEER_HALF), lambda i: (0, 0)),
                  pl.BlockSpec((nq, PEER_N_KEYS, PEER_HALF), lambda i: (0, 0, 0))],
        out_specs=[pl.BlockSpec((tm, PEER_SLOTS), lambda i: (i, 0))] * 2,
        out_shape=[jax.ShapeDtypeStruct((T, PEER_SLOTS), jnp.int32),
                   jax.ShapeDtypeStruct((T, PEER_SLOTS), jnp.float32)],
        scratch_shapes=[pltpu.VMEM((PEER_N_KEYS * PEER_HEADS, tm), jnp.float32),
                        pltpu.VMEM((PEER_N_KEYS * PEER_HEADS, tm), jnp.float32),
                        pltpu.VMEM((2, PEER_TOPK) + tile, jnp.float32),
                        pltpu.VMEM((2, PEER_TOPK) + tile, jnp.int32),
                        pltpu.VMEM((len(_CANDIDATES),) + tile, jnp.float32),
                        pltpu.VMEM((len(_CANDIDATES),) + tile, jnp.int32),
                        pltpu.VMEM((PEER_TOPK,) + tile, jnp.float32),
                        pltpu.VMEM((PEER_TOPK,) + tile, jnp.int32)],
        compiler_params=pltpu.CompilerParams(dimension_semantics=("arbitrary",),
                                             vmem_limit_bytes=VMEM_LIMIT),
        name="route",
    )(h2, wq_bf16, sk_bf16)


def _expert_kernel(idx_ref, gate_ref, h_ref, x1_ref, gf_ref, uv_hbm, o_ref, buf0, buf1, sem):
    i = pl.program_id(0)
    n = pl.num_programs(0)
    nsub = EXPERT_TILE // EXPERT_SUB
    rows = EXPERT_SUB * PEER_SLOTS
    bufs = (buf0, buf1)

    def issue(sub, slot):
        base = sub * rows

        def group(m, carry):
            r0 = m * PEER_TOPK
            q0 = m * (PEER_TOPK // SUBLANES)
            for j in range(PEER_TOPK):
                e = idx_ref[0, base + r0 + j]
                dst = bufs[slot].at[q0 + j // SUBLANES, pl.ds(j % SUBLANES, 1), :]
                pltpu.make_async_copy(uv_hbm.at[e], dst, sem.at[slot]).start()
            return carry
        lax.fori_loop(0, rows // PEER_TOPK, group, 0)

    def wait(slot):
        pltpu.make_async_copy(bufs[slot], bufs[slot], sem.at[slot]).wait()

    @pl.when(i == 0)
    def _():
        issue(0, 0)

    row_id = lax.broadcasted_iota(jnp.int32, (EXPERT_SUB, PEER_SLOTS), 0)
    row_wide = lax.broadcasted_iota(jnp.int32, (EXPERT_SUB, rows), 0)
    blk_wide = lax.broadcasted_iota(jnp.int32, (EXPERT_SUB, rows), 1) // PEER_SLOTS

    def consume(s, slot):
        tok = pl.ds(pl.multiple_of(s * EXPERT_SUB, EXPERT_SUB), EXPERT_SUB)
        u = bufs[slot][:, :, pl.ds(0, D_MODEL)].reshape(rows, D_MODEL).astype(jnp.bfloat16)
        a = lax.dot_general(h_ref[tok, :], u, _NT, preferred_element_type=jnp.float32)
        act = jnp.zeros((EXPERT_SUB, PEER_SLOTS), jnp.float32)
        for c in range(EXPERT_SUB):
            act = act + jnp.where(row_id == c, a[:, c * PEER_SLOTS:(c + 1) * PEER_SLOTS], 0.0)
        wgt = gate_ref[tok, :] * _gelu_exact(act)
        wide = jnp.where(blk_wide == row_wide, jnp.concatenate([wgt] * EXPERT_SUB, axis=1), 0.0)
        v = bufs[slot][:, :, pl.ds(D_MODEL, D_MODEL)].reshape(rows, D_MODEL).astype(jnp.bfloat16)
        y = jnp.dot(wide.astype(jnp.bfloat16), v, preferred_element_type=jnp.float32)
        o_ref[tok, :] = _rms(x1_ref[tok, :] + y, gf_ref[...])

    def sub_pair(p, carry):
        for slot in range(2):
            s = 2 * p + slot

            @pl.when(jnp.logical_or(s + 1 < nsub, i + 1 < n))
            def _():
                issue(s + 1, 1 - slot)

            wait(slot)
            consume(s, slot)
        return carry

    lax.fori_loop(0, nsub // 2, sub_pair, 0)


def _experts(idx, gate, h2, x1, g_final, uv):
    T = idx.shape[0]
    n = T // EXPERT_TILE
    rows = EXPERT_SUB * PEER_SLOTS
    tile = lambda w: pl.BlockSpec((EXPERT_TILE, w), lambda i: (i, 0))
    idx3 = idx.reshape(n, EXPERT_TILE * PEER_SLOTS)
    idx_ext = jnp.concatenate([idx3, jnp.roll(idx3[:, :rows], -1, axis=0)], axis=1)
    return pl.pallas_call(
        _expert_kernel,
        grid=(n,),
        in_specs=[pl.BlockSpec((None, 1, (EXPERT_TILE + EXPERT_SUB) * PEER_SLOTS), lambda i: (i, 0, 0),
                               memory_space=pltpu.SMEM),
                  tile(PEER_SLOTS), tile(D_MODEL), tile(D_MODEL),
                  pl.BlockSpec((1, D_MODEL), lambda i: (0, 0)),
                  pl.BlockSpec(memory_space=pl.ANY)],
        out_specs=tile(D_MODEL),
        out_shape=jax.ShapeDtypeStruct((T, D_MODEL), jnp.float32),
        scratch_shapes=[pltpu.VMEM((rows // SUBLANES, SUBLANES, 2 * D_MODEL), jnp.float32),
                        pltpu.VMEM((rows // SUBLANES, SUBLANES, 2 * D_MODEL), jnp.float32),
                        pltpu.SemaphoreType.DMA((2,))],
        compiler_params=pltpu.CompilerParams(dimension_semantics=("arbitrary",),
                                             vmem_limit_bytes=VMEM_LIMIT,
                                             disable_bounds_checks=True),
        name="experts",
    )(idx_ext[:, None, :], gate, h2, x1, g_final, uv[:, None, :])


def _layer(x, g_mix, w_in, w_pool, pool_scale, w_out, g_ffn, w_peer_q, peer_subkeys, peer_u, peer_v,
           g_out):
    B, S, D = x.shape
    T = B * S
    bf = jnp.bfloat16
    x2 = x.reshape(T, D)
    proj3 = _inproj(x2, g_mix.reshape(1, D), w_in.astype(bf)).reshape(B, S, IN_PROJ_WIDTH)

    w_bd = jnp.zeros((POOL_WIDTH, POOL_WIDTH), jnp.float32)
    for gi in range(len(POOL_WINDOWS)):
        sl = slice(gi * POOL_GROUP, (gi + 1) * POOL_GROUP)
        w_bd = w_bd.at[sl, sl].set(w_pool[gi])
    pool2 = _pool(proj3, w_bd.astype(bf), pool_scale.reshape(1, POOL_WIDTH)).reshape(T, POOL_WIDTH)

    outs, lses = [], []
    for gi in range(len(ATTN_CONFIGS)):
        o_lo, o_hi, l_lo, l_hi = _attention(proj3, gi)
        outs += [o_lo.reshape(T, LANES), o_hi.reshape(T, LANES)]
        lses += [l_lo.reshape(T, LANES), l_hi.reshape(T, LANES)]

    x1, h2 = _outproj(x2, pool2, outs, lses, w_out.astype(bf), g_ffn.reshape(1, D))
    sk = peer_subkeys.reshape(2 * PEER_HEADS, PEER_N_KEYS, PEER_HALF).astype(bf)
    idx, gate = _route(h2, w_peer_q.astype(bf), sk)
    uv = jnp.concatenate([peer_u, peer_v], axis=1)
    y = _experts(idx, gate, h2, x1, g_out.reshape(1, D), uv)
    return y.reshape(B, S, D)


def kernel(x, g_mix, w_in, w_pool, pool_scale, w_out, g_ffn, w_peer_q, peer_subkeys, peer_u, peer_v,
           g_final):
    assert g_mix.shape[0] == 1, "single-layer block"
    return _layer(x, g_mix[0], w_in[0], w_pool[0], pool_scale[0], w_out[0], g_ffn[0], w_peer_q[0],
                  peer_subkeys[0], peer_u[0], peer_v[0], g_final)
```

```python
import functools
import math

import numpy as np
import jax
import jax.numpy as jnp
from jax import lax
from jax.experimental import pallas as pl
from jax.experimental.pallas import tpu as pltpu

D_MODEL = 1024
LANES = 128
SUBLANES = 8
HEAD_DIM = 64
POOL_WINDOWS = (2, 4, 8, 16)
POOL_GROUP = 64
POOL_WIDTH = POOL_GROUP * len(POOL_WINDOWS)
ATTN_CONFIGS = ((128, 1), (512, 4), (2048, 16))
HEADS_PER_GROUP = 4
GROUP_WIDTH = HEADS_PER_GROUP * HEAD_DIM
N_ATTN_HEADS = HEADS_PER_GROUP * len(ATTN_CONFIGS)
ATTN_WIDTH = N_ATTN_HEADS * HEAD_DIM
IN_PROJ_WIDTH = POOL_WIDTH + 3 * ATTN_WIDTH
ATTN_HALF = 64
PEER_HEADS = 8
PEER_N_KEYS = 128
PEER_HALF = 128
PEER_TOPK = 16
PEER_SLOTS = PEER_HEADS * PEER_TOPK
RMS_EPS = 1e-6
NEG_INF = -1e30

VMEM_LIMIT = 48 * 1024 * 1024
POOL_PAD_LO = 16
POOL_PAD = 48
QUERY_BLOCK = 128
EXPERT_SUB = 8
EXPERT_RING = 4
EXPERT_AHEAD = 2

_NT = (((1,), (1,)), ((), ()))


def _alibi_slopes():
    return np.asarray(2.0 ** (-8.0 * np.arange(1, N_ATTN_HEADS + 1) / N_ATTN_HEADS), dtype=np.float32)


def _rms(x, g):
    return x * lax.rsqrt(jnp.mean(x * x, axis=-1, keepdims=True) + RMS_EPS) * g


def _gelu_exact(x):
    return 0.5 * x * (1.0 + lax.erf(x * math.sqrt(0.5)))


def _inproj_kernel(x_ref, g_ref, w_ref, o_ref):
    h = _rms(x_ref[...], g_ref[...])
    o_ref[...] = jnp.dot(h.astype(jnp.bfloat16), w_ref[...], preferred_element_type=jnp.float32)


def _inproj(x2, g, w_bf16, tm=512):
    T = x2.shape[0]
    return pl.pallas_call(
        _inproj_kernel,
        grid=(T // tm,),
        in_specs=[pl.BlockSpec((tm, D_MODEL), lambda i: (i, 0)),
                  pl.BlockSpec((1, D_MODEL), lambda i: (0, 0)),
                  pl.BlockSpec((D_MODEL, IN_PROJ_WIDTH), lambda i: (0, 0))],
        out_specs=pl.BlockSpec((tm, IN_PROJ_WIDTH), lambda i: (i, 0)),
        out_shape=jax.ShapeDtypeStruct((T, IN_PROJ_WIDTH), jnp.float32),
        compiler_params=pltpu.CompilerParams(dimension_semantics=("arbitrary",),
                                             vmem_limit_bytes=VMEM_LIMIT),
        name="inproj",
    )(x2, g, w_bf16)


def _pool_kernel(p_ref, w_ref, sc_ref, o_ref, a_ref, b_ref, c_ref):
    S = p_ref.shape[1]
    p = p_ref[0]
    lo = POOL_PAD_LO
    a_ref[pl.ds(0, lo), :] = jnp.zeros((lo, POOL_WIDTH), jnp.float32)
    a_ref[pl.ds(lo + S, POOL_PAD - lo), :] = jnp.zeros((POOL_PAD - lo, POOL_WIDTH), jnp.float32)
    a_ref[pl.ds(lo, S), :] = p
    n2, n4, n8 = S + 40, S + 32, S + 24
    b_ref[pl.ds(0, n2), :] = a_ref[pl.ds(0, n2), :] + a_ref[pl.ds(1, n2), :]
    c_ref[pl.ds(0, n4), :] = b_ref[pl.ds(0, n4), :] + b_ref[pl.ds(2, n4), :]
    w2 = b_ref[pl.ds(lo - 1, S), :]
    w4 = c_ref[pl.ds(lo - 2, S), :]
    b_ref[pl.ds(0, n8), :] = c_ref[pl.ds(0, n8), :] + c_ref[pl.ds(4, n8), :]
    w8 = b_ref[pl.ds(lo - 4, S), :]
    w16 = b_ref[pl.ds(lo - 8, S), :] + b_ref[pl.ds(lo, S), :]
    lane = lax.broadcasted_iota(jnp.int32, (S, POOL_WIDTH), 1)
    t = lax.broadcasted_iota(jnp.int32, (S, POOL_WIDTH), 0)
    g0, g1, g2 = lane < POOL_GROUP, lane < 2 * POOL_GROUP, lane < 3 * POOL_GROUP
    wsum = jnp.where(g0, w2, jnp.where(g1, w4, jnp.where(g2, w8, w16)))
    half = jnp.where(g0, 1, jnp.where(g1, 2, jnp.where(g2, 4, 8)))
    cnt = jnp.minimum(t + half, S) - jnp.maximum(t - half, 0)
    pooled = wsum / cnt.astype(jnp.float32) - p
    mixed = jnp.dot(pooled.astype(jnp.bfloat16), w_ref[...], preferred_element_type=jnp.float32)
    o_ref[0] = mixed * sc_ref[...]


def _pool(proj3, w_blockdiag, scale):
    B, S, _ = proj3.shape
    return pl.pallas_call(
        _pool_kernel,
        grid=(B,),
        in_specs=[pl.BlockSpec((1, S, POOL_WIDTH), lambda b: (b, 0, 0)),
                  pl.BlockSpec((POOL_WIDTH, POOL_WIDTH), lambda b: (0, 0)),
                  pl.BlockSpec((1, POOL_WIDTH), lambda b: (0, 0))],
        out_specs=pl.BlockSpec((1, S, POOL_WIDTH), lambda b: (b, 0, 0)),
        out_shape=jax.ShapeDtypeStruct((B, S, POOL_WIDTH), jnp.float32),
        scratch_shapes=[pltpu.VMEM((S + POOL_PAD, POOL_WIDTH), jnp.float32)] * 3,
        compiler_params=pltpu.CompilerParams(dimension_semantics=("arbitrary",),
                                             vmem_limit_bytes=VMEM_LIMIT),
        name="pool",
    )(proj3, w_blockdiag, scale)


def _attn_kernel(q0, q1, k0, k1, v0, v1, o0, o1, l0, l1, qs, ks, vs, os_, ls, *, dilation, slopes):
    S = q0.shape[1]
    d = dilation
    L = S // d
    QB = min(QUERY_BLOCK, L)
    W = min(L, QB + 2 * ATTN_HALF)
    nb = L // QB
    for r in range(d):
        rows = pl.ds(r, L, stride=d) if d > 1 else pl.ds(0, L)
        for dst, halves in ((qs, (q0, q1)), (ks, (k0, k1)), (vs, (v0, v1))):
            for c, src in enumerate(halves):
                dst[r, :, pl.ds(c * LANES, LANES)] = src[0, rows, :].astype(jnp.bfloat16)

    H = HEADS_PER_GROUP
    head_of_lane = lax.broadcasted_iota(jnp.int32, (QB, GROUP_WIDTH), 1) // HEAD_DIM
    stacked_row = lax.broadcasted_iota(jnp.int32, (H * QB, W), 0)
    col_minus_row = lax.broadcasted_iota(jnp.int32, (H * QB, W), 1) - stacked_row % QB
    head_of_row = lax.broadcasted_iota(jnp.int32, (H * QB, 1), 0) // QB
    slope_of_row = functools.reduce(
        lambda acc, h: jnp.where(head_of_row == h, slopes[h], acc), range(1, H),
        jnp.full((H * QB, 1), slopes[0], jnp.float32))
    scale = 1.0 / math.sqrt(HEAD_DIM)

    def block(n, carry):
        r = n // nb
        i0 = pl.multiple_of((n % nb) * QB, QB)
        ws = pl.multiple_of(jnp.clip(i0 - ATTN_HALF, 0, L - W), ATTN_HALF)
        q = qs[r, pl.ds(i0, QB), :]
        k = ks[r, pl.ds(ws, W), :]
        v = vs[r, pl.ds(ws, W), :]
        arel = jnp.abs(col_minus_row + (ws - i0))
        dist = (d * arel).astype(jnp.float32)
        q_stacked = jnp.concatenate(
            [jnp.where(head_of_lane == h, q, jnp.zeros_like(q)) for h in range(H)], axis=0)
        s = lax.dot_general(q_stacked, k, _NT, preferred_element_type=jnp.float32) * scale
        s = jnp.where(arel <= ATTN_HALF, s - slope_of_row * dist, NEG_INF)
        m = jnp.max(s, axis=-1, keepdims=True)
        pexp = jnp.exp(s - m)
        den = jnp.sum(pexp, axis=-1, keepdims=True)
        lse = m + jnp.log(den)
        o_stacked = jnp.dot(pexp.astype(jnp.bfloat16), v, preferred_element_type=jnp.float32) / den
        o_acc = o_stacked[0:QB]
        l_acc = jnp.broadcast_to(lse[0:QB], (QB, GROUP_WIDTH))
        for h in range(1, H):
            mine = head_of_lane == h
            o_acc = jnp.where(mine, o_stacked[h * QB:(h + 1) * QB], o_acc)
            l_acc = jnp.where(mine, lse[h * QB:(h + 1) * QB], l_acc)
        os_[r, pl.ds(i0, QB), :] = o_acc
        ls[r, pl.ds(i0, QB), :] = l_acc
        return carry

    lax.fori_loop(0, d * nb, block, 0)
    for r in range(d):
        rows = pl.ds(r, L, stride=d) if d > 1 else pl.ds(0, L)
        for c, (o_ref, l_ref) in enumerate(((o0, l0), (o1, l1))):
            o_ref[0, rows, :] = os_[r, :, pl.ds(c * LANES, LANES)]
            l_ref[0, rows, :] = ls[r, :, pl.ds(c * LANES, LANES)]


def _attention(proj3, group):
    B, S, _ = proj3.shape
    _, d = ATTN_CONFIGS[group]
    L = S // d
    slopes = tuple(float(s) for s in _alibi_slopes()[group * HEADS_PER_GROUP:(group + 1) * HEADS_PER_GROUP])
    per_group = GROUP_WIDTH // LANES
    first = POOL_WIDTH // LANES + group * per_group
    step = len(ATTN_CONFIGS) * per_group
    spec = lambda cb: pl.BlockSpec((1, S, LANES), lambda b: (b, 0, cb))
    out_spec = pl.BlockSpec((1, S, LANES), lambda b: (b, 0, 0))
    return pl.pallas_call(
        functools.partial(_attn_kernel, dilation=d, slopes=slopes),
        grid=(B,),
        in_specs=[spec(first + t * step + c) for t in range(3) for c in range(per_group)],
        out_specs=[out_spec] * 4,
        out_shape=[jax.ShapeDtypeStruct((B, S, LANES), jnp.float32)] * 4,
        scratch_shapes=[pltpu.VMEM((d, L, GROUP_WIDTH), jnp.bfloat16)] * 3
                       + [pltpu.VMEM((d, L, GROUP_WIDTH), jnp.float32)] * 2,
        compiler_params=pltpu.CompilerParams(dimension_semantics=("arbitrary",),
                                             vmem_limit_bytes=VMEM_LIMIT),
        name=f"attn_d{d}",
    )(*([proj3] * (3 * per_group)))


def _outproj_kernel(x_ref, pool_ref, *refs):
    ng = len(ATTN_CONFIGS)
    per_group = GROUP_WIDTH // LANES
    o_refs, l_refs = refs[:ng * per_group], refs[ng * per_group:2 * ng * per_group]
    w_ref, g_ref, x1_ref, h_ref = refs[2 * ng * per_group:]
    acc = x_ref[...] + jnp.dot(pool_ref[...].astype(jnp.bfloat16), w_ref[pl.ds(0, POOL_WIDTH), :],
                               preferred_element_type=jnp.float32)
    for c in range(per_group):
        lse = [l_refs[g * per_group + c][...] for g in range(ng)]
        m = functools.reduce(jnp.maximum, lse)
        e = [jnp.exp(l - m) for l in lse]
        tot = functools.reduce(jnp.add, e)
        for g in range(ng):
            part = o_refs[g * per_group + c][...] * (e[g] / tot)
            row0 = POOL_WIDTH + g * GROUP_WIDTH + c * LANES
            acc = acc + jnp.dot(part.astype(jnp.bfloat16), w_ref[pl.ds(row0, LANES), :],
                                preferred_element_type=jnp.float32)
    x1_ref[...] = acc
    h_ref[...] = _rms(acc, g_ref[...]).astype(jnp.bfloat16)


def _outproj(x2, pool2, os_, ls_, w_bf16, g, tm=512):
    T = x2.shape[0]
    wide = pl.BlockSpec((tm, D_MODEL), lambda i: (i, 0))
    return pl.pallas_call(
        _outproj_kernel,
        grid=(T // tm,),
        in_specs=[wide, pl.BlockSpec((tm, POOL_WIDTH), lambda i: (i, 0))]
                 + [pl.BlockSpec((tm, LANES), lambda i: (i, 0))] * (len(os_) + len(ls_))
                 + [pl.BlockSpec((D_MODEL, D_MODEL), lambda i: (0, 0)),
                    pl.BlockSpec((1, D_MODEL), lambda i: (0, 0))],
        out_specs=[wide, wide],
        out_shape=[jax.ShapeDtypeStruct((T, D_MODEL), jnp.float32),
                   jax.ShapeDtypeStruct((T, D_MODEL), jnp.bfloat16)],
        compiler_params=pltpu.CompilerParams(dimension_semantics=("arbitrary",),
                                             vmem_limit_bytes=VMEM_LIMIT),
        name="outproj",
    )(x2, pool2, *os_, *ls_, w_bf16, g)


_CANDIDATES = tuple((a, b) for a in range(PEER_TOPK) for b in range(PEER_TOPK)
                    if (a + 1) * (b + 1) <= PEER_TOPK)


def _reduce(op, xs, ways=4):
    acc = []
    for i, x in enumerate(xs):
        if i < ways:
            acc.append(x)
        else:
            acc[i % ways] = op(acc[i % ways], x)
    while len(acc) > 1:
        acc = [op(acc[i], acc[i + 1]) if i + 1 < len(acc) else acc[i] for i in range(0, len(acc), 2)]
    return acc[0]


def _topk_rows(row, n, k, emit):
    def removed(pos):
        for r in range(n):
            v = jnp.where(pos == r, -jnp.inf, row(r))
            row(r, v)
            yield v

    def body(i, m):
        pos = _reduce(jnp.minimum, (jnp.where(row(r) == m, r, n) for r in range(n)))
        emit(i, m, pos)
        return _reduce(jnp.maximum, removed(pos))

    lax.fori_loop(0, k, body, _reduce(jnp.maximum, (row(r) for r in range(n))))


def _route_kernel(h_ref, wq_ref, sk_ref, idx_ref, gate_ref, s0_scr, s1_scr, sv_scr, si_scr, cs_scr, ci_scr,
                  ts_scr, ex_scr):
    tm = h_ref.shape[0]
    q = jnp.dot(h_ref[...], wq_ref[...], preferred_element_type=jnp.float32).astype(jnp.bfloat16)
    s_scr = (s0_scr, s1_scr)
    for hd in range(PEER_HEADS):
        for p in range(2):
            c = 2 * hd + p
            s = lax.dot_general(sk_ref[c], q[:, c * PEER_HALF:(c + 1) * PEER_HALF], _NT,
                                preferred_element_type=jnp.float32)
            s_scr[p][pl.ds(hd, PEER_N_KEYS, stride=PEER_HEADS), :] = s

    for p in range(2):
        def row(r, new=None, p=p):
            if new is None:
                return s_scr[p][pl.ds(r * PEER_HEADS, PEER_HEADS), :]
            s_scr[p][pl.ds(r * PEER_HEADS, PEER_HEADS), :] = new

        def emit(i, value, r, p=p):
            sv_scr[p, i] = value
            si_scr[p, i] = r

        _topk_rows(row, PEER_N_KEYS, PEER_TOPK, emit)

    for r, (a, b) in enumerate(_CANDIDATES):
        cs_scr[r] = sv_scr[0, a] + sv_scr[1, b]
        ci_scr[r] = si_scr[0, a] * PEER_N_KEYS + si_scr[1, b]

    def cand(r, new=None):
        if new is None:
            return cs_scr[r]
        cs_scr[r] = new

    def emit_expert(i, value, r):
        ts_scr[i] = value
        ex_scr[i] = _reduce(jnp.maximum, (jnp.where(r == c, ci_scr[c], -1) for c in range(len(_CANDIDATES))))

    _topk_rows(cand, len(_CANDIDATES), PEER_TOPK, emit_expert)

    top_s = ts_scr[...]
    e = jnp.exp(top_s - top_s[0:1])
    gate = e / jnp.sum(e, axis=0, keepdims=True)
    idx_ref[...] = ex_scr[...].reshape(PEER_SLOTS, tm).T
    gate_ref[...] = gate.reshape(PEER_SLOTS, tm).T


def _route(h2, wq_bf16, sk_bf16):
    T = h2.shape[0]
    tm = LANES
    nq = 2 * PEER_HEADS
    tile = (PEER_HEADS, tm)
    assert PEER_HEADS == SUBLANES
    return pl.pallas_call(
        _route_kernel,
        grid=(T // tm,),
        in_specs=[pl.BlockSpec((tm, D_MODEL), lambda i: (i, 0)),
                  pl.BlockSpec((D_MODEL, nq * PEER_HALF), lambda i: (0, 0)),
                  pl.BlockSpec((nq, PEER_N_KEYS, PEER_HALF), lambda i: (0, 0, 0))],
        out_specs=[pl.BlockSpec((tm, PEER_SLOTS), lambda i: (i, 0))] * 2,
        out_shape=[jax.ShapeDtypeStruct((T, PEER_SLOTS), jnp.int32),
                   jax.ShapeDtypeStruct((T, PEER_SLOTS), jnp.float32)],
        scratch_shapes=[pltpu.VMEM((PEER_N_KEYS * PEER_HEADS, tm), jnp.float32),
                        pltpu.VMEM((PEER_N_KEYS * PEER_HEADS, tm), jnp.float32),
                        pltpu.VMEM((2, PEER_TOPK) + tile, jnp.float32),
                        pltpu.VMEM((2, PEER_TOPK) + tile, jnp.int32),
                        pltpu.VMEM((len(_CANDIDATES),) + tile, jnp.float32),
                        pltpu.VMEM((len(_CANDIDATES),) + tile, jnp.int32),
                        pltpu.VMEM((PEER_TOPK,) + tile, jnp.float32),
                        pltpu.VMEM((PEER_TOPK,) + tile, jnp.int32)],
        compiler_params=pltpu.CompilerParams(dimension_semantics=("arbitrary",),
                                             vmem_limit_bytes=VMEM_LIMIT),
        name="route",
    )(h2, wq_bf16, sk_bf16)


def _expert_kernel(idx_ref, gate_ref, h_ref, x1_ref, gf_ref, uv_hbm, o_ref, *scratch):
    bufs, sem = scratch[:EXPERT_RING], scratch[EXPERT_RING]
    i = pl.program_id(0)
    n = pl.num_programs(0)
    rows = EXPERT_SUB * PEER_SLOTS

    def issue(sub):
        slot = sub % EXPERT_RING
        for r in range(rows):
            e = idx_ref[0, sub * rows + r]
            dst = bufs[slot].at[r // SUBLANES, pl.ds(r % SUBLANES, 1), :]
            pltpu.make_async_copy(uv_hbm.at[e], dst, sem.at[slot]).start(priority=r % 2)

    def wait(slot):
        pltpu.make_async_copy(bufs[slot], bufs[slot], sem.at[slot]).wait()

    @pl.when(i == 0)
    def _():
        for sub in range(EXPERT_AHEAD):
            issue(sub)

    row_id = lax.broadcasted_iota(jnp.int32, (EXPERT_SUB, PEER_SLOTS), 0)
    row_wide = lax.broadcasted_iota(jnp.int32, (EXPERT_SUB, rows), 0)
    blk_wide = lax.broadcasted_iota(jnp.int32, (EXPERT_SUB, rows), 1) // PEER_SLOTS
    h_all = h_ref[...].astype(jnp.float32)

    for sub in range(EXPERT_RING):
        tok = pl.ds(sub * EXPERT_SUB, EXPERT_SUB)
        buf = bufs[sub]
        wait(sub)
        issue(sub + EXPERT_AHEAD)
        h = h_all[sub * EXPERT_SUB:(sub + 1) * EXPERT_SUB].astype(jnp.bfloat16)
        u = buf[:, :, pl.ds(0, D_MODEL)].reshape(rows, D_MODEL).astype(jnp.bfloat16)
        a = lax.dot_general(h, u, _NT, preferred_element_type=jnp.float32)
        act = jnp.zeros((EXPERT_SUB, PEER_SLOTS), jnp.float32)
        for c in range(EXPERT_SUB):
            act = act + jnp.where(row_id == c, a[:, c * PEER_SLOTS:(c + 1) * PEER_SLOTS], 0.0)
        wgt = gate_ref[tok, :] * _gelu_exact(act)
        wide = jnp.where(blk_wide == row_wide, jnp.concatenate([wgt] * EXPERT_SUB, axis=1), 0.0)
        v = buf[:, :, pl.ds(D_MODEL, D_MODEL)].reshape(rows, D_MODEL).astype(jnp.bfloat16)
        y = jnp.dot(wide.astype(jnp.bfloat16), v, preferred_element_type=jnp.float32)
        o_ref[tok, :] = _rms(x1_ref[tok, :] + y, gf_ref[...])

    @pl.when(i == n - 1)
    def _():
        for sub in range(EXPERT_AHEAD):
            wait(sub)


def _experts(idx, gate, h2, x1, g_final, uv):
    T = idx.shape[0]
    tile_tokens = EXPERT_RING * EXPERT_SUB
    n = T // tile_tokens
    rows = EXPERT_SUB * PEER_SLOTS
    tile = lambda w: pl.BlockSpec((tile_tokens, w), lambda i: (i, 0))
    idx3 = idx.reshape(n, tile_tokens * PEER_SLOTS)
    idx_ext = jnp.concatenate([idx3, jnp.roll(idx3[:, :EXPERT_AHEAD * rows], -1, axis=0)], axis=1)
    buf = pltpu.VMEM((rows // SUBLANES, SUBLANES, 2 * D_MODEL), jnp.float32)
    return pl.pallas_call(
        _expert_kernel,
        grid=(n,),
        in_specs=[pl.BlockSpec((None, 1, (tile_tokens + EXPERT_AHEAD * EXPERT_SUB) * PEER_SLOTS),
                               lambda i: (i, 0, 0), memory_space=pltpu.SMEM),
                  tile(PEER_SLOTS), tile(D_MODEL), tile(D_MODEL),
                  pl.BlockSpec((1, D_MODEL), lambda i: (0, 0)),
                  pl.BlockSpec(memory_space=pl.ANY)],
        out_specs=tile(D_MODEL),
        out_shape=jax.ShapeDtypeStruct((T, D_MODEL), jnp.float32),
        scratch_shapes=[buf] * EXPERT_RING + [pltpu.SemaphoreType.DMA((EXPERT_RING,))],
        compiler_params=pltpu.CompilerParams(dimension_semantics=("arbitrary",),
                                             vmem_limit_bytes=VMEM_LIMIT,
                                             disable_bounds_checks=True),
        name="experts",
    )(idx_ext[:, None, :], gate, h2, x1, g_final, uv[:, None, :])


def _layer(x, g_mix, w_in, w_pool, pool_scale, w_out, g_ffn, w_peer_q, peer_subkeys, peer_u, peer_v,
           g_out):
    B, S, D = x.shape
    T = B * S
    bf = jnp.bfloat16
    x2 = x.reshape(T, D)
    proj3 = _inproj(x2, g_mix.reshape(1, D), w_in.astype(bf)).reshape(B, S, IN_PROJ_WIDTH)

    w_bd = jnp.zeros((POOL_WIDTH, POOL_WIDTH), jnp.float32)
    for gi in range(len(POOL_WINDOWS)):
        sl = slice(gi * POOL_GROUP, (gi + 1) * POOL_GROUP)
        w_bd = w_bd.at[sl, sl].set(w_pool[gi])
    pool2 = _pool(proj3, w_bd.astype(bf), pool_scale.reshape(1, POOL_WIDTH)).reshape(T, POOL_WIDTH)

    outs, lses = [], []
    for gi in range(len(ATTN_CONFIGS)):
        o_lo, o_hi, l_lo, l_hi = _attention(proj3, gi)
        outs += [o_lo.reshape(T, LANES), o_hi.reshape(T, LANES)]
        lses += [l_lo.reshape(T, LANES), l_hi.reshape(T, LANES)]

    x1, h2 = _outproj(x2, pool2, outs, lses, w_out.astype(bf), g_ffn.reshape(1, D))
    sk = peer_subkeys.reshape(2 * PEER_HEADS, PEER_N_KEYS, PEER_HALF).astype(bf)
    idx, gate = _route(h2, w_peer_q.astype(bf), sk)
    uv = jnp.concatenate([peer_u, peer_v], axis=1)
    y = _experts(idx, gate, h2, x1, g_out.reshape(1, D), uv)
    return y.reshape(B, S, D)


def kernel(x, g_mix, w_in, w_pool, pool_scale, w_out, g_ffn, w_peer_q, peer_subkeys, peer_u, peer_v,
           g_final):
    assert g_mix.shape[0] == 1, "single-layer block"
    return _layer(x, g_mix[0], w_in[0], w_pool[0], pool_scale[0], w_out[0], g_ffn[0], w_peer_q[0],
                  peer_subkeys[0], peer_u[0], peer_v[0], g_final)
```

```python
import functools
import math

import numpy as np
import jax
import jax.numpy as jnp
from jax import lax
from jax.experimental import pallas as pl
from jax.experimental.pallas import tpu as pltpu

D_MODEL = 1024
LANES = 128
SUBLANES = 8
HEAD_DIM = 64
POOL_WINDOWS = (2, 4, 8, 16)
POOL_GROUP = 64
POOL_WIDTH = POOL_GROUP * len(POOL_WINDOWS)
ATTN_CONFIGS = ((128, 1), (512, 4), (2048, 16))
HEADS_PER_GROUP = 4
GROUP_WIDTH = HEADS_PER_GROUP * HEAD_DIM
N_ATTN_HEADS = HEADS_PER_GROUP * len(ATTN_CONFIGS)
ATTN_WIDTH = N_ATTN_HEADS * HEAD_DIM
IN_PROJ_WIDTH = POOL_WIDTH + 3 * ATTN_WIDTH
ATTN_HALF = 64
PEER_HEADS = 8
PEER_N_KEYS = 128
PEER_HALF = 128
PEER_TOPK = 16
PEER_SLOTS = PEER_HEADS * PEER_TOPK
RMS_EPS = 1e-6
NEG_INF = -1e30

VMEM_LIMIT = 48 * 1024 * 1024
POOL_PAD_LO = 16
POOL_PAD = 48
QUERY_BLOCK = 128
EXPERT_SUB = 8
EXPERT_RING = 4
EXPERT_AHEAD = 2
ROW_TILES = 2 * D_MODEL // LANES
ROW_PITCH = 20

_NT = (((1,), (1,)), ((), ()))


def _alibi_slopes():
    return np.asarray(2.0 ** (-8.0 * np.arange(1, N_ATTN_HEADS + 1) / N_ATTN_HEADS), dtype=np.float32)


def _rms(x, g):
    return x * lax.rsqrt(jnp.mean(x * x, axis=-1, keepdims=True) + RMS_EPS) * g


def _gelu_exact(x):
    return 0.5 * x * (1.0 + lax.erf(x * math.sqrt(0.5)))


def _inproj_kernel(x_ref, g_ref, w_ref, o_ref):
    h = _rms(x_ref[...], g_ref[...])
    o_ref[...] = jnp.dot(h.astype(jnp.bfloat16), w_ref[...], preferred_element_type=jnp.float32)


def _inproj(x2, g, w_bf16, tm=512):
    T = x2.shape[0]
    return pl.pallas_call(
        _inproj_kernel,
        grid=(T // tm,),
        in_specs=[pl.BlockSpec((tm, D_MODEL), lambda i: (i, 0)),
                  pl.BlockSpec((1, D_MODEL), lambda i: (0, 0)),
                  pl.BlockSpec((D_MODEL, IN_PROJ_WIDTH), lambda i: (0, 0))],
        out_specs=pl.BlockSpec((tm, IN_PROJ_WIDTH), lambda i: (i, 0)),
        out_shape=jax.ShapeDtypeStruct((T, IN_PROJ_WIDTH), jnp.float32),
        compiler_params=pltpu.CompilerParams(dimension_semantics=("arbitrary",),
                                             vmem_limit_bytes=VMEM_LIMIT),
        name="inproj",
    )(x2, g, w_bf16)


def _pool_kernel(p_ref, w_ref, sc_ref, o_ref, a_ref, b_ref, c_ref):
    S = p_ref.shape[1]
    p = p_ref[0]
    lo = POOL_PAD_LO
    a_ref[pl.ds(0, lo), :] = jnp.zeros((lo, POOL_WIDTH), jnp.float32)
    a_ref[pl.ds(lo + S, POOL_PAD - lo), :] = jnp.zeros((POOL_PAD - lo, POOL_WIDTH), jnp.float32)
    a_ref[pl.ds(lo, S), :] = p
    n2, n4, n8 = S + 40, S + 32, S + 24
    b_ref[pl.ds(0, n2), :] = a_ref[pl.ds(0, n2), :] + a_ref[pl.ds(1, n2), :]
    c_ref[pl.ds(0, n4), :] = b_ref[pl.ds(0, n4), :] + b_ref[pl.ds(2, n4), :]
    w2 = b_ref[pl.ds(lo - 1, S), :]
    w4 = c_ref[pl.ds(lo - 2, S), :]
    b_ref[pl.ds(0, n8), :] = c_ref[pl.ds(0, n8), :] + c_ref[pl.ds(4, n8), :]
    w8 = b_ref[pl.ds(lo - 4, S), :]
    w16 = b_ref[pl.ds(lo - 8, S), :] + b_ref[pl.ds(lo, S), :]
    lane = lax.broadcasted_iota(jnp.int32, (S, POOL_WIDTH), 1)
    t = lax.broadcasted_iota(jnp.int32, (S, POOL_WIDTH), 0)
    g0, g1, g2 = lane < POOL_GROUP, lane < 2 * POOL_GROUP, lane < 3 * POOL_GROUP
    wsum = jnp.where(g0, w2, jnp.where(g1, w4, jnp.where(g2, w8, w16)))
    half = jnp.where(g0, 1, jnp.where(g1, 2, jnp.where(g2, 4, 8)))
    cnt = jnp.minimum(t + half, S) - jnp.maximum(t - half, 0)
    pooled = wsum / cnt.astype(jnp.float32) - p
    mixed = jnp.dot(pooled.astype(jnp.bfloat16), w_ref[...], preferred_element_type=jnp.float32)
    o_ref[0] = mixed * sc_ref[...]


def _pool(proj3, w_blockdiag, scale):
    B, S, _ = proj3.shape
    return pl.pallas_call(
        _pool_kernel,
        grid=(B,),
        in_specs=[pl.BlockSpec((1, S, POOL_WIDTH), lambda b: (b, 0, 0)),
                  pl.BlockSpec((POOL_WIDTH, POOL_WIDTH), lambda b: (0, 0)),
                  pl.BlockSpec((1, POOL_WIDTH), lambda b: (0, 0))],
        out_specs=pl.BlockSpec((1, S, POOL_WIDTH), lambda b: (b, 0, 0)),
        out_shape=jax.ShapeDtypeStruct((B, S, POOL_WIDTH), jnp.float32),
        scratch_shapes=[pltpu.VMEM((S + POOL_PAD, POOL_WIDTH), jnp.float32)] * 3,
        compiler_params=pltpu.CompilerParams(dimension_semantics=("arbitrary",),
                                             vmem_limit_bytes=VMEM_LIMIT),
        name="pool",
    )(proj3, w_blockdiag, scale)


def _attn_kernel(q0, q1, k0, k1, v0, v1, o0, o1, l0, l1, qs, ks, vs, os_, ls, *, dilation, slopes):
    S = q0.shape[1]
    d = dilation
    L = S // d
    QB = min(QUERY_BLOCK, L)
    W = min(L, QB + 2 * ATTN_HALF)
    nb = L // QB
    for r in range(d):
        rows = pl.ds(r, L, stride=d) if d > 1 else pl.ds(0, L)
        for dst, halves in ((qs, (q0, q1)), (ks, (k0, k1)), (vs, (v0, v1))):
            for c, src in enumerate(halves):
                dst[r, :, pl.ds(c * LANES, LANES)] = src[0, rows, :].astype(jnp.bfloat16)

    H = HEADS_PER_GROUP
    head_of_lane = lax.broadcasted_iota(jnp.int32, (QB, GROUP_WIDTH), 1) // HEAD_DIM
    stacked_row = lax.broadcasted_iota(jnp.int32, (H * QB, W), 0)
    col_minus_row = lax.broadcasted_iota(jnp.int32, (H * QB, W), 1) - stacked_row % QB
    head_of_row = lax.broadcasted_iota(jnp.int32, (H * QB, 1), 0) // QB
    slope_of_row = functools.reduce(
        lambda acc, h: jnp.where(head_of_row == h, slopes[h], acc), range(1, H),
        jnp.full((H * QB, 1), slopes[0], jnp.float32))
    scale = 1.0 / math.sqrt(HEAD_DIM)

    def block(n, carry):
        r = n // nb
        i0 = pl.multiple_of((n % nb) * QB, QB)
        ws = pl.multiple_of(jnp.clip(i0 - ATTN_HALF, 0, L - W), ATTN_HALF)
        q = qs[r, pl.ds(i0, QB), :]
        k = ks[r, pl.ds(ws, W), :]
        v = vs[r, pl.ds(ws, W), :]
        arel = jnp.abs(col_minus_row + (ws - i0))
        dist = (d * arel).astype(jnp.float32)
        q_stacked = jnp.concatenate(
            [jnp.where(head_of_lane == h, q, jnp.zeros_like(q)) for h in range(H)], axis=0)
        s = lax.dot_general(q_stacked, k, _NT, preferred_element_type=jnp.float32) * scale
        s = jnp.where(arel <= ATTN_HALF, s - slope_of_row * dist, NEG_INF)
        m = jnp.max(s, axis=-1, keepdims=True)
        pexp = jnp.exp(s - m)
        den = jnp.sum(pexp, axis=-1, keepdims=True)
        lse = m + jnp.log(den)
        o_stacked = jnp.dot(pexp.astype(jnp.bfloat16), v, preferred_element_type=jnp.float32) / den
        o_acc = o_stacked[0:QB]
        l_acc = jnp.broadcast_to(lse[0:QB], (QB, GROUP_WIDTH))
        for h in range(1, H):
            mine = head_of_lane == h
            o_acc = jnp.where(mine, o_stacked[h * QB:(h + 1) * QB], o_acc)
            l_acc = jnp.where(mine, lse[h * QB:(h + 1) * QB], l_acc)
        os_[r, pl.ds(i0, QB), :] = o_acc
        ls[r, pl.ds(i0, QB), :] = l_acc
        return carry

    lax.fori_loop(0, d * nb, block, 0)
    for r in range(d):
        rows = pl.ds(r, L, stride=d) if d > 1 else pl.ds(0, L)
        for c, (o_ref, l_ref) in enumerate(((o0, l0), (o1, l1))):
            o_ref[0, rows, :] = os_[r, :, pl.ds(c * LANES, LANES)]
            l_ref[0, rows, :] = ls[r, :, pl.ds(c * LANES, LANES)]


def _attention(proj3, group):
    B, S, _ = proj3.shape
    _, d = ATTN_CONFIGS[group]
    L = S // d
    slopes = tuple(float(s) for s in _alibi_slopes()[group * HEADS_PER_GROUP:(group + 1) * HEADS_PER_GROUP])
    per_group = GROUP_WIDTH // LANES
    first = POOL_WIDTH // LANES + group * per_group
    step = len(ATTN_CONFIGS) * per_group
    spec = lambda cb: pl.BlockSpec((1, S, LANES), lambda b: (b, 0, cb))
    out_spec = pl.BlockSpec((1, S, LANES), lambda b: (b, 0, 0))
    return pl.pallas_call(
        functools.partial(_attn_kernel, dilation=d, slopes=slopes),
        grid=(B,),
        in_specs=[spec(first + t * step + c) for t in range(3) for c in range(per_group)],
        out_specs=[out_spec] * 4,
        out_shape=[jax.ShapeDtypeStruct((B, S, LANES), jnp.float32)] * 4,
        scratch_shapes=[pltpu.VMEM((d, L, GROUP_WIDTH), jnp.bfloat16)] * 3
                       + [pltpu.VMEM((d, L, GROUP_WIDTH), jnp.float32)] * 2,
        compiler_params=pltpu.CompilerParams(dimension_semantics=("arbitrary",),
                                             vmem_limit_bytes=VMEM_LIMIT),
        name=f"attn_d{d}",
    )(*([proj3] * (3 * per_group)))


def _outproj_kernel(x_ref, pool_ref, *refs):
    ng = len(ATTN_CONFIGS)
    per_group = GROUP_WIDTH // LANES
    o_refs, l_refs = refs[:ng * per_group], refs[ng * per_group:2 * ng * per_group]
    w_ref, g_ref, x1_ref, h_ref = refs[2 * ng * per_group:]
    acc = x_ref[...] + jnp.dot(pool_ref[...].astype(jnp.bfloat16), w_ref[pl.ds(0, POOL_WIDTH), :],
                               preferred_element_type=jnp.float32)
    for c in range(per_group):
        lse = [l_refs[g * per_group + c][...] for g in range(ng)]
        m = functools.reduce(jnp.maximum, lse)
        e = [jnp.exp(l - m) for l in lse]
        tot = functools.reduce(jnp.add, e)
        for g in range(ng):
            part = o_refs[g * per_group + c][...] * (e[g] / tot)
            row0 = POOL_WIDTH + g * GROUP_WIDTH + c * LANES
            acc = acc + jnp.dot(part.astype(jnp.bfloat16), w_ref[pl.ds(row0, LANES), :],
                                preferred_element_type=jnp.float32)
    x1_ref[...] = acc
    h_ref[...] = _rms(acc, g_ref[...]).astype(jnp.bfloat16)


def _outproj(x2, pool2, os_, ls_, w_bf16, g, tm=512):
    T = x2.shape[0]
    wide = pl.BlockSpec((tm, D_MODEL), lambda i: (i, 0))
    return pl.pallas_call(
        _outproj_kernel,
        grid=(T // tm,),
        in_specs=[wide, pl.BlockSpec((tm, POOL_WIDTH), lambda i: (i, 0))]
                 + [pl.BlockSpec((tm, LANES), lambda i: (i, 0))] * (len(os_) + len(ls_))
                 + [pl.BlockSpec((D_MODEL, D_MODEL), lambda i: (0, 0)),
                    pl.BlockSpec((1, D_MODEL), lambda i: (0, 0))],
        out_specs=[wide, wide],
        out_shape=[jax.ShapeDtypeStruct((T, D_MODEL), jnp.float32),
                   jax.ShapeDtypeStruct((T, D_MODEL), jnp.bfloat16)],
        compiler_params=pltpu.CompilerParams(dimension_semantics=("arbitrary",),
                                             vmem_limit_bytes=VMEM_LIMIT),
        name="outproj",
    )(x2, pool2, *os_, *ls_, w_bf16, g)


_CANDIDATES = tuple((a, b) for a in range(PEER_TOPK) for b in range(PEER_TOPK)
                    if (a + 1) * (b + 1) <= PEER_TOPK)


def _reduce(op, xs, ways=4):
    acc = []
    for i, x in enumerate(xs):
        if i < ways:
            acc.append(x)
        else:
            acc[i % ways] = op(acc[i % ways], x)
    while len(acc) > 1:
        acc = [op(acc[i], acc[i + 1]) if i + 1 < len(acc) else acc[i] for i in range(0, len(acc), 2)]
    return acc[0]


def _first_max(values, ways=4):
    best = []
    for r, v in values:
        if len(best) < ways:
            best.append((v, jnp.full(v.shape, r, jnp.int32)))
        else:
            m, p = best[r % ways]
            later = v > m
            best[r % ways] = (jnp.where(later, v, m), jnp.where(later, r, p))

    def merge(a, b):
        take_b = (b[0] > a[0]) | ((b[0] == a[0]) & (b[1] < a[1]))
        return jnp.where(take_b, b[0], a[0]), jnp.where(take_b, b[1], a[1])

    while len(best) > 1:
        best = [merge(best[i], best[i + 1]) if i + 1 < len(best) else best[i]
                for i in range(0, len(best), 2)]
    return best[0]


def _topk_rows(row, n, k, emit):
    def removed(pos):
        for r in range(n):
            v = jnp.where(pos == r, -jnp.inf, row(r))
            row(r, v)
            yield r, v

    def body(i, carry):
        m, pos = carry
        emit(i, m, pos)
        return _first_max(removed(pos))

    lax.fori_loop(0, k, body, _first_max((r, row(r)) for r in range(n)))


def _route_kernel(h_ref, wq_ref, sk_ref, idx_ref, gate_ref, s0_scr, s1_scr, sv_scr, si_scr, cs_scr, ci_scr,
                  ts_scr, ex_scr):
    tm = h_ref.shape[0]
    q = jnp.dot(h_ref[...], wq_ref[...], preferred_element_type=jnp.float32).astype(jnp.bfloat16)
    s_scr = (s0_scr, s1_scr)
    for hd in range(PEER_HEADS):
        for p in range(2):
            c = 2 * hd + p
            s = lax.dot_general(sk_ref[c], q[:, c * PEER_HALF:(c + 1) * PEER_HALF], _NT,
                                preferred_element_type=jnp.float32)
            s_scr[p][pl.ds(hd, PEER_N_KEYS, stride=PEER_HEADS), :] = s

    for p in range(2):
        def row(r, new=None, p=p):
            if new is None:
                return s_scr[p][pl.ds(r * PEER_HEADS, PEER_HEADS), :]
            s_scr[p][pl.ds(r * PEER_HEADS, PEER_HEADS), :] = new

        def emit(i, value, r, p=p):
            sv_scr[p, i] = value
            si_scr[p, i] = r

        _topk_rows(row, PEER_N_KEYS, PEER_TOPK, emit)

    for r, (a, b) in enumerate(_CANDIDATES):
        cs_scr[r] = sv_scr[0, a] + sv_scr[1, b]
        ci_scr[r] = si_scr[0, a] * PEER_N_KEYS + si_scr[1, b]

    def cand(r, new=None):
        if new is None:
            return cs_scr[r]
        cs_scr[r] = new

    def emit_expert(i, value, r):
        ts_scr[i] = value
        ex_scr[i] = _reduce(jnp.maximum, (jnp.where(r == c, ci_scr[c], -1) for c in range(len(_CANDIDATES))))

    _topk_rows(cand, len(_CANDIDATES), PEER_TOPK, emit_expert)

    top_s = ts_scr[...]
    e = jnp.exp(top_s - top_s[0:1])
    gate = e / jnp.sum(e, axis=0, keepdims=True)
    idx_ref[...] = ex_scr[...].reshape(PEER_SLOTS, tm).T
    gate_ref[...] = gate.reshape(PEER_SLOTS, tm).T


def _route(h2, wq_bf16, sk_bf16):
    T = h2.shape[0]
    tm = LANES
    nq = 2 * PEER_HEADS
    tile = (PEER_HEADS, tm)
    assert PEER_HEADS == SUBLANES
    return pl.pallas_call(
        _route_kernel,
        grid=(T // tm,),
        in_specs=[pl.BlockSpec((tm, D_MODEL), lambda i: (i, 0)),
                  pl.BlockSpec((D_MODEL, nq * PEER_HALF), lambda i: (0, 0)),
                  pl.BlockSpec((nq, PEER_N_KEYS, PEER_HALF), lambda i: (0, 0, 0))],
        out_specs=[pl.BlockSpec((tm, PEER_SLOTS), lambda i: (i, 0))] * 2,
        out_shape=[jax.ShapeDtypeStruct((T, PEER_SLOTS), jnp.int32),
                   jax.ShapeDtypeStruct((T, PEER_SLOTS), jnp.float32)],
        scratch_shapes=[pltpu.VMEM((PEER_N_KEYS * PEER_HEADS, tm), jnp.float32),
                        pltpu.VMEM((PEER_N_KEYS * PEER_HEADS, tm), jnp.float32),
                        pltpu.VMEM((2, PEER_TOPK) + tile, jnp.float32),
                        pltpu.VMEM((2, PEER_TOPK) + tile, jnp.int32),
                        pltpu.VMEM((len(_CANDIDATES),) + tile, jnp.float32),
                        pltpu.VMEM((len(_CANDIDATES),) + tile, jnp.int32),
                        pltpu.VMEM((PEER_TOPK,) + tile, jnp.float32),
                        pltpu.VMEM((PEER_TOPK,) + tile, jnp.int32)],
        compiler_params=pltpu.CompilerParams(dimension_semantics=("arbitrary",),
                                             vmem_limit_bytes=VMEM_LIMIT),
        name="route",
    )(h2, wq_bf16, sk_bf16)


def _expert_kernel(idx_ref, gate_ref, h_ref, x1_ref, gf_ref, uv_hbm, o_ref, *scratch):
    bufs, sem = scratch[:EXPERT_RING], scratch[EXPERT_RING]
    i = pl.program_id(0)
    n = pl.num_programs(0)
    rows = EXPERT_SUB * PEER_SLOTS

    def issue(sub):
        slot = sub % EXPERT_RING
        for r in range(rows):
            e = idx_ref[0, sub * rows + r]
            dst = bufs[slot].at[pl.ds(r * ROW_PITCH, ROW_TILES), :]
            pltpu.make_async_copy(uv_hbm.at[e], dst, sem.at[slot]).start(priority=r % 2)

    def matrix(buf, first_tile):
        tiles = [buf[pl.ds(first_tile + c, rows, stride=ROW_PITCH), :].astype(jnp.bfloat16)
                 for c in range(D_MODEL // LANES)]
        return jnp.concatenate(tiles, axis=1)

    def wait(slot):
        copied = bufs[slot].at[pl.ds(0, rows * ROW_TILES), :]
        pltpu.make_async_copy(copied, copied, sem.at[slot]).wait()

    @pl.when(i == 0)
    def _():
        for sub in range(EXPERT_AHEAD):
            issue(sub)

    row_id = lax.broadcasted_iota(jnp.int32, (EXPERT_SUB, PEER_SLOTS), 0)
    row_wide = lax.broadcasted_iota(jnp.int32, (EXPERT_SUB, rows), 0)
    blk_wide = lax.broadcasted_iota(jnp.int32, (EXPERT_SUB, rows), 1) // PEER_SLOTS
    h_all = h_ref[...].astype(jnp.float32)

    for sub in range(EXPERT_RING):
        tok = pl.ds(sub * EXPERT_SUB, EXPERT_SUB)
        buf = bufs[sub]
        wait(sub)
        issue(sub + EXPERT_AHEAD)
        h = h_all[sub * EXPERT_SUB:(sub + 1) * EXPERT_SUB].astype(jnp.bfloat16)
        a = lax.dot_general(h, matrix(buf, 0), _NT, preferred_element_type=jnp.float32)
        act = jnp.zeros((EXPERT_SUB, PEER_SLOTS), jnp.float32)
        for c in range(EXPERT_SUB):
            act = act + jnp.where(row_id == c, a[:, c * PEER_SLOTS:(c + 1) * PEER_SLOTS], 0.0)
        wgt = gate_ref[tok, :] * _gelu_exact(act)
        wide = jnp.where(blk_wide == row_wide, jnp.concatenate([wgt] * EXPERT_SUB, axis=1), 0.0)
        y = jnp.dot(wide.astype(jnp.bfloat16), matrix(buf, D_MODEL // LANES),
                    preferred_element_type=jnp.float32)
        o_ref[tok, :] = _rms(x1_ref[tok, :] + y, gf_ref[...])

    @pl.when(i == n - 1)
    def _():
        for sub in range(EXPERT_AHEAD):
            wait(sub)


def _experts(idx, gate, h2, x1, g_final, uv):
    T = idx.shape[0]
    tile_tokens = EXPERT_RING * EXPERT_SUB
    n = T // tile_tokens
    rows = EXPERT_SUB * PEER_SLOTS
    tile = lambda w: pl.BlockSpec((tile_tokens, w), lambda i: (i, 0))
    idx3 = idx.reshape(n, tile_tokens * PEER_SLOTS)
    idx_ext = jnp.concatenate([idx3, jnp.roll(idx3[:, :EXPERT_AHEAD * rows], -1, axis=0)], axis=1)
    buf = pltpu.VMEM((rows * ROW_PITCH, LANES), jnp.float32)
    return pl.pallas_call(
        _expert_kernel,
        grid=(n,),
        in_specs=[pl.BlockSpec((None, 1, (tile_tokens + EXPERT_AHEAD * EXPERT_SUB) * PEER_SLOTS),
                               lambda i: (i, 0, 0), memory_space=pltpu.SMEM),
                  tile(PEER_SLOTS), tile(D_MODEL), tile(D_MODEL),
                  pl.BlockSpec((1, D_MODEL), lambda i: (0, 0)),
                  pl.BlockSpec(memory_space=pl.ANY)],
        out_specs=tile(D_MODEL),
        out_shape=jax.ShapeDtypeStruct((T, D_MODEL), jnp.float32),
        scratch_shapes=[buf] * EXPERT_RING + [pltpu.SemaphoreType.DMA((EXPERT_RING,))],
        compiler_params=pltpu.CompilerParams(dimension_semantics=("arbitrary",),
                                             vmem_limit_bytes=VMEM_LIMIT,
                                             disable_bounds_checks=True),
        name="experts",
    )(idx_ext[:, None, :], gate, h2, x1, g_final, uv.reshape(-1, ROW_TILES, LANES))


def _layer(x, g_mix, w_in, w_pool, pool_scale, w_out, g_ffn, w_peer_q, peer_subkeys, peer_u, peer_v,
           g_out):
    B, S, D = x.shape
    T = B * S
    bf = jnp.bfloat16
    x2 = x.reshape(T, D)
    proj3 = _inproj(x2, g_mix.reshape(1, D), w_in.astype(bf)).reshape(B, S, IN_PROJ_WIDTH)

    w_bd = jnp.zeros((POOL_WIDTH, POOL_WIDTH), jnp.float32)
    for gi in range(len(POOL_WINDOWS)):
        sl = slice(gi * POOL_GROUP, (gi + 1) * POOL_GROUP)
        w_bd = w_bd.at[sl, sl].set(w_pool[gi])
    pool2 = _pool(proj3, w_bd.astype(bf), pool_scale.reshape(1, POOL_WIDTH)).reshape(T, POOL_WIDTH)

    outs, lses = [], []
    for gi in range(len(ATTN_CONFIGS)):
        o_lo, o_hi, l_lo, l_hi = _attention(proj3, gi)
        outs += [o_lo.reshape(T, LANES), o_hi.reshape(T, LANES)]
        lses += [l_lo.reshape(T, LANES), l_hi.reshape(T, LANES)]

    x1, h2 = _outproj(x2, pool2, outs, lses, w_out.astype(bf), g_ffn.reshape(1, D))
    sk = peer_subkeys.reshape(2 * PEER_HEADS, PEER_N_KEYS, PEER_HALF).astype(bf)
    idx, gate = _route(h2, w_peer_q.astype(bf), sk)
    uv = jnp.concatenate([peer_u, peer_v], axis=1)
    y = _experts(idx, gate, h2, x1, g_out.reshape(1, D), uv)
    return y.reshape(B, S, D)


def kernel(x, g_mix, w_in, w_pool, pool_scale, w_out, g_ffn, w_peer_q, peer_subkeys, peer_u, peer_v,
           g_final):
    assert g_mix.shape[0] == 1, "single-layer block"
    return _layer(x, g_mix[0], w_in[0], w_pool[0], pool_scale[0], w_out[0], g_ffn[0], w_peer_q[0],
                  peer_subkeys[0], peer_u[0], peer_v[0], g_final)
```

```python
import functools
import math

import numpy as np
import jax
import jax.numpy as jnp
from jax import lax
from jax.experimental import pallas as pl
from jax.experimental.pallas import tpu as pltpu

D_MODEL = 1024
LANES = 128
SUBLANES = 8
HEAD_DIM = 64
POOL_WINDOWS = (2, 4, 8, 16)
POOL_GROUP = 64
POOL_WIDTH = POOL_GROUP * len(POOL_WINDOWS)
ATTN_CONFIGS = ((128, 1), (512, 4), (2048, 16))
HEADS_PER_GROUP = 4
GROUP_WIDTH = HEADS_PER_GROUP * HEAD_DIM
N_ATTN_HEADS = HEADS_PER_GROUP * len(ATTN_CONFIGS)
ATTN_WIDTH = N_ATTN_HEADS * HEAD_DIM
IN_PROJ_WIDTH = POOL_WIDTH + 3 * ATTN_WIDTH
ATTN_HALF = 64
PEER_HEADS = 8
PEER_N_KEYS = 128
PEER_HALF = 128
PEER_TOPK = 16
PEER_SLOTS = PEER_HEADS * PEER_TOPK
RMS_EPS = 1e-6
NEG_INF = -1e30

VMEM_LIMIT = 48 * 1024 * 1024
POOL_PAD_LO = 16
POOL_PAD = 48
QUERY_BLOCK = 128
EXPERT_SUB = 8
EXPERT_RING = 4
EXPERT_AHEAD = 2
ROW_TILES = D_MODEL // LANES
ROW_PITCH = 12
PACK_BLOCK = 256

_NT = (((1,), (1,)), ((), ()))


def _alibi_slopes():
    return np.asarray(2.0 ** (-8.0 * np.arange(1, N_ATTN_HEADS + 1) / N_ATTN_HEADS), dtype=np.float32)


def _rms(x, g):
    return x * lax.rsqrt(jnp.mean(x * x, axis=-1, keepdims=True) + RMS_EPS) * g


def _gelu_exact(x):
    return 0.5 * x * (1.0 + lax.erf(x * math.sqrt(0.5)))


def _inproj_kernel(x_ref, g_ref, w_ref, o_ref):
    h = _rms(x_ref[...], g_ref[...])
    o_ref[...] = jnp.dot(h.astype(jnp.bfloat16), w_ref[...], preferred_element_type=jnp.float32)


def _inproj(x2, g, w_bf16, tm=512):
    T = x2.shape[0]
    return pl.pallas_call(
        _inproj_kernel,
        grid=(T // tm,),
        in_specs=[pl.BlockSpec((tm, D_MODEL), lambda i: (i, 0)),
                  pl.BlockSpec((1, D_MODEL), lambda i: (0, 0)),
                  pl.BlockSpec((D_MODEL, IN_PROJ_WIDTH), lambda i: (0, 0))],
        out_specs=pl.BlockSpec((tm, IN_PROJ_WIDTH), lambda i: (i, 0)),
        out_shape=jax.ShapeDtypeStruct((T, IN_PROJ_WIDTH), jnp.float32),
        compiler_params=pltpu.CompilerParams(dimension_semantics=("arbitrary",),
                                             vmem_limit_bytes=VMEM_LIMIT),
        name="inproj",
    )(x2, g, w_bf16)


def _pool_kernel(p_ref, w_ref, sc_ref, o_ref, a_ref, b_ref, c_ref):
    S = p_ref.shape[1]
    p = p_ref[0]
    lo = POOL_PAD_LO
    a_ref[pl.ds(0, lo), :] = jnp.zeros((lo, POOL_WIDTH), jnp.float32)
    a_ref[pl.ds(lo + S, POOL_PAD - lo), :] = jnp.zeros((POOL_PAD - lo, POOL_WIDTH), jnp.float32)
    a_ref[pl.ds(lo, S), :] = p
    n2, n4, n8 = S + 40, S + 32, S + 24
    b_ref[pl.ds(0, n2), :] = a_ref[pl.ds(0, n2), :] + a_ref[pl.ds(1, n2), :]
    c_ref[pl.ds(0, n4), :] = b_ref[pl.ds(0, n4), :] + b_ref[pl.ds(2, n4), :]
    w2 = b_ref[pl.ds(lo - 1, S), :]
    w4 = c_ref[pl.ds(lo - 2, S), :]
    b_ref[pl.ds(0, n8), :] = c_ref[pl.ds(0, n8), :] + c_ref[pl.ds(4, n8), :]
    w8 = b_ref[pl.ds(lo - 4, S), :]
    w16 = b_ref[pl.ds(lo - 8, S), :] + b_ref[pl.ds(lo, S), :]
    lane = lax.broadcasted_iota(jnp.int32, (S, POOL_WIDTH), 1)
    t = lax.broadcasted_iota(jnp.int32, (S, POOL_WIDTH), 0)
    g0, g1, g2 = lane < POOL_GROUP, lane < 2 * POOL_GROUP, lane < 3 * POOL_GROUP
    wsum = jnp.where(g0, w2, jnp.where(g1, w4, jnp.where(g2, w8, w16)))
    half = jnp.where(g0, 1, jnp.where(g1, 2, jnp.where(g2, 4, 8)))
    cnt = jnp.minimum(t + half, S) - jnp.maximum(t - half, 0)
    pooled = wsum / cnt.astype(jnp.float32) - p
    mixed = jnp.dot(pooled.astype(jnp.bfloat16), w_ref[...], preferred_element_type=jnp.float32)
    o_ref[0] = mixed * sc_ref[...]


def _pool(proj3, w_blockdiag, scale):
    B, S, _ = proj3.shape
    return pl.pallas_call(
        _pool_kernel,
        grid=(B,),
        in_specs=[pl.BlockSpec((1, S, POOL_WIDTH), lambda b: (b, 0, 0)),
                  pl.BlockSpec((POOL_WIDTH, POOL_WIDTH), lambda b: (0, 0)),
                  pl.BlockSpec((1, POOL_WIDTH), lambda b: (0, 0))],
        out_specs=pl.BlockSpec((1, S, POOL_WIDTH), lambda b: (b, 0, 0)),
        out_shape=jax.ShapeDtypeStruct((B, S, POOL_WIDTH), jnp.float32),
        scratch_shapes=[pltpu.VMEM((S + POOL_PAD, POOL_WIDTH), jnp.float32)] * 3,
        compiler_params=pltpu.CompilerParams(dimension_semantics=("arbitrary",),
                                             vmem_limit_bytes=VMEM_LIMIT),
        name="pool",
    )(proj3, w_blockdiag, scale)


def _attn_kernel(q0, q1, k0, k1, v0, v1, o0, o1, l0, l1, qs, ks, vs, os_, ls, *, dilation, slopes):
    S = q0.shape[1]
    d = dilation
    L = S // d
    QB = min(QUERY_BLOCK, L)
    W = min(L, QB + 2 * ATTN_HALF)
    nb = L // QB
    for r in range(d):
        rows = pl.ds(r, L, stride=d) if d > 1 else pl.ds(0, L)
        for dst, halves in ((qs, (q0, q1)), (ks, (k0, k1)), (vs, (v0, v1))):
            for c, src in enumerate(halves):
                dst[r, :, pl.ds(c * LANES, LANES)] = src[0, rows, :].astype(jnp.bfloat16)

    H = HEADS_PER_GROUP
    head_of_lane = lax.broadcasted_iota(jnp.int32, (QB, GROUP_WIDTH), 1) // HEAD_DIM
    stacked_row = lax.broadcasted_iota(jnp.int32, (H * QB, W), 0)
    col_minus_row = lax.broadcasted_iota(jnp.int32, (H * QB, W), 1) - stacked_row % QB
    head_of_row = lax.broadcasted_iota(jnp.int32, (H * QB, 1), 0) // QB
    slope_of_row = functools.reduce(
        lambda acc, h: jnp.where(head_of_row == h, slopes[h], acc), range(1, H),
        jnp.full((H * QB, 1), slopes[0], jnp.float32))
    scale = 1.0 / math.sqrt(HEAD_DIM)

    def block(n, carry):
        r = n // nb
        i0 = pl.multiple_of((n % nb) * QB, QB)
        ws = pl.multiple_of(jnp.clip(i0 - ATTN_HALF, 0, L - W), ATTN_HALF)
        q = qs[r, pl.ds(i0, QB), :]
        k = ks[r, pl.ds(ws, W), :]
        v = vs[r, pl.ds(ws, W), :]
        arel = jnp.abs(col_minus_row + (ws - i0))
        dist = (d * arel).astype(jnp.float32)
        q_stacked = jnp.concatenate(
            [jnp.where(head_of_lane == h, q, jnp.zeros_like(q)) for h in range(H)], axis=0)
        s = lax.dot_general(q_stacked, k, _NT, preferred_element_type=jnp.float32) * scale
        s = jnp.where(arel <= ATTN_HALF, s - slope_of_row * dist, NEG_INF)
        m = jnp.max(s, axis=-1, keepdims=True)
        pexp = jnp.exp(s - m)
        den = jnp.sum(pexp, axis=-1, keepdims=True)
        lse = m + jnp.log(den)
        o_stacked = jnp.dot(pexp.astype(jnp.bfloat16), v, preferred_element_type=jnp.float32) / den
        o_acc = o_stacked[0:QB]
        l_acc = jnp.broadcast_to(lse[0:QB], (QB, GROUP_WIDTH))
        for h in range(1, H):
            mine = head_of_lane == h
            o_acc = jnp.where(mine, o_stacked[h * QB:(h + 1) * QB], o_acc)
            l_acc = jnp.where(mine, lse[h * QB:(h + 1) * QB], l_acc)
        os_[r, pl.ds(i0, QB), :] = o_acc
        ls[r, pl.ds(i0, QB), :] = l_acc
        return carry

    lax.fori_loop(0, d * nb, block, 0)
    for r in range(d):
        rows = pl.ds(r, L, stride=d) if d > 1 else pl.ds(0, L)
        for c, (o_ref, l_ref) in enumerate(((o0, l0), (o1, l1))):
            o_ref[0, rows, :] = os_[r, :, pl.ds(c * LANES, LANES)]
            l_ref[0, rows, :] = ls[r, :, pl.ds(c * LANES, LANES)]


def _attention(proj3, group):
    B, S, _ = proj3.shape
    _, d = ATTN_CONFIGS[group]
    L = S // d
    slopes = tuple(float(s) for s in _alibi_slopes()[group * HEADS_PER_GROUP:(group + 1) * HEADS_PER_GROUP])
    per_group = GROUP_WIDTH // LANES
    first = POOL_WIDTH // LANES + group * per_group
    step = len(ATTN_CONFIGS) * per_group
    spec = lambda cb: pl.BlockSpec((1, S, LANES), lambda b: (b, 0, cb))
    out_spec = pl.BlockSpec((1, S, LANES), lambda b: (b, 0, 0))
    return pl.pallas_call(
        functools.partial(_attn_kernel, dilation=d, slopes=slopes),
        grid=(B,),
        in_specs=[spec(first + t * step + c) for t in range(3) for c in range(per_group)],
        out_specs=[out_spec] * 4,
        out_shape=[jax.ShapeDtypeStruct((B, S, LANES), jnp.float32)] * 4,
        scratch_shapes=[pltpu.VMEM((d, L, GROUP_WIDTH), jnp.bfloat16)] * 3
                       + [pltpu.VMEM((d, L, GROUP_WIDTH), jnp.float32)] * 2,
        compiler_params=pltpu.CompilerParams(dimension_semantics=("arbitrary",),
                                             vmem_limit_bytes=VMEM_LIMIT),
        name=f"attn_d{d}",
    )(*([proj3] * (3 * per_group)))


def _outproj_kernel(x_ref, pool_ref, *refs):
    ng = len(ATTN_CONFIGS)
    per_group = GROUP_WIDTH // LANES
    o_refs, l_refs = refs[:ng * per_group], refs[ng * per_group:2 * ng * per_group]
    w_ref, g_ref, x1_ref, h_ref = refs[2 * ng * per_group:]
    acc = x_ref[...] + jnp.dot(pool_ref[...].astype(jnp.bfloat16), w_ref[pl.ds(0, POOL_WIDTH), :],
                               preferred_element_type=jnp.float32)
    for c in range(per_group):
        lse = [l_refs[g * per_group + c][...] for g in range(ng)]
        m = functools.reduce(jnp.maximum, lse)
        e = [jnp.exp(l - m) for l in lse]
        tot = functools.reduce(jnp.add, e)
        for g in range(ng):
            part = o_refs[g * per_group + c][...] * (e[g] / tot)
            row0 = POOL_WIDTH + g * GROUP_WIDTH + c * LANES
            acc = acc + jnp.dot(part.astype(jnp.bfloat16), w_ref[pl.ds(row0, LANES), :],
                                preferred_element_type=jnp.float32)
    x1_ref[...] = acc
    h_ref[...] = _rms(acc, g_ref[...]).astype(jnp.bfloat16)


def _outproj(x2, pool2, os_, ls_, w_bf16, g, tm=512):
    T = x2.shape[0]
    wide = pl.BlockSpec((tm, D_MODEL), lambda i: (i, 0))
    return pl.pallas_call(
        _outproj_kernel,
        grid=(T // tm,),
        in_specs=[wide, pl.BlockSpec((tm, POOL_WIDTH), lambda i: (i, 0))]
                 + [pl.BlockSpec((tm, LANES), lambda i: (i, 0))] * (len(os_) + len(ls_))
                 + [pl.BlockSpec((D_MODEL, D_MODEL), lambda i: (0, 0)),
                    pl.BlockSpec((1, D_MODEL), lambda i: (0, 0))],
        out_specs=[wide, wide],
        out_shape=[jax.ShapeDtypeStruct((T, D_MODEL), jnp.float32),
                   jax.ShapeDtypeStruct((T, D_MODEL), jnp.bfloat16)],
        compiler_params=pltpu.CompilerParams(dimension_semantics=("arbitrary",),
                                             vmem_limit_bytes=VMEM_LIMIT),
        name="outproj",
    )(x2, pool2, *os_, *ls_, w_bf16, g)


_CANDIDATES = tuple((a, b) for a in range(PEER_TOPK) for b in range(PEER_TOPK)
                    if (a + 1) * (b + 1) <= PEER_TOPK)


def _reduce(op, xs, ways=4):
    acc = []
    for i, x in enumerate(xs):
        if i < ways:
            acc.append(x)
        else:
            acc[i % ways] = op(acc[i % ways], x)
    while len(acc) > 1:
        acc = [op(acc[i], acc[i + 1]) if i + 1 < len(acc) else acc[i] for i in range(0, len(acc), 2)]
    return acc[0]


def _first_max(values, ways=4):
    best = []
    for r, v in values:
        if len(best) < ways:
            best.append((v, jnp.full(v.shape, r, jnp.int32)))
        else:
            m, p = best[r % ways]
            later = v > m
            best[r % ways] = (jnp.where(later, v, m), jnp.where(later, r, p))

    def merge(a, b):
        take_b = (b[0] > a[0]) | ((b[0] == a[0]) & (b[1] < a[1]))
        return jnp.where(take_b, b[0], a[0]), jnp.where(take_b, b[1], a[1])

    while len(best) > 1:
        best = [merge(best[i], best[i + 1]) if i + 1 < len(best) else best[i]
                for i in range(0, len(best), 2)]
    return best[0]


def _topk_rows(row, n, k, emit):
    def removed(pos):
        for r in range(n):
            v = jnp.where(pos == r, -jnp.inf, row(r))
            row(r, v)
            yield r, v

    def body(i, carry):
        m, pos = carry
        emit(i, m, pos)
        return _first_max(removed(pos))

    lax.fori_loop(0, k, body, _first_max((r, row(r)) for r in range(n)))


def _route_kernel(h_ref, wq_ref, sk_ref, idx_ref, gate_ref, s0_scr, s1_scr, sv_scr, si_scr, cs_scr, ci_scr,
                  ts_scr, ex_scr):
    tm = h_ref.shape[0]
    q = jnp.dot(h_ref[...], wq_ref[...], preferred_element_type=jnp.float32).astype(jnp.bfloat16)
    s_scr = (s0_scr, s1_scr)
    for hd in range(PEER_HEADS):
        for p in range(2):
            c = 2 * hd + p
            s = lax.dot_general(sk_ref[c], q[:, c * PEER_HALF:(c + 1) * PEER_HALF], _NT,
                                preferred_element_type=jnp.float32)
            s_scr[p][pl.ds(hd, PEER_N_KEYS, stride=PEER_HEADS), :] = s

    for p in range(2):
        def row(r, new=None, p=p):
            if new is None:
                return s_scr[p][pl.ds(r * PEER_HEADS, PEER_HEADS), :]
            s_scr[p][pl.ds(r * PEER_HEADS, PEER_HEADS), :] = new

        def emit(i, value, r, p=p):
            sv_scr[p, i] = value
            si_scr[p, i] = r

        _topk_rows(row, PEER_N_KEYS, PEER_TOPK, emit)

    for r, (a, b) in enumerate(_CANDIDATES):
        cs_scr[r] = sv_scr[0, a] + sv_scr[1, b]
        ci_scr[r] = si_scr[0, a] * PEER_N_KEYS + si_scr[1, b]

    def cand(r, new=None):
        if new is None:
            return cs_scr[r]
        cs_scr[r] = new

    def emit_expert(i, value, r):
        ts_scr[i] = value
        ex_scr[i] = _reduce(jnp.maximum, (jnp.where(r == c, ci_scr[c], -1) for c in range(len(_CANDIDATES))))

    _topk_rows(cand, len(_CANDIDATES), PEER_TOPK, emit_expert)

    top_s = ts_scr[...]
    e = jnp.exp(top_s - top_s[0:1])
    gate = e / jnp.sum(e, axis=0, keepdims=True)
    idx_ref[...] = ex_scr[...].reshape(PEER_SLOTS, tm).T
    gate_ref[...] = gate.reshape(PEER_SLOTS, tm).T


def _route(h2, wq_bf16, sk_bf16):
    T = h2.shape[0]
    tm = LANES
    nq = 2 * PEER_HEADS
    tile = (PEER_HEADS, tm)
    assert PEER_HEADS == SUBLANES
    return pl.pallas_call(
        _route_kernel,
        grid=(T // tm,),
        in_specs=[pl.BlockSpec((tm, D_MODEL), lambda i: (i, 0)),
                  pl.BlockSpec((D_MODEL, nq * PEER_HALF), lambda i: (0, 0)),
                  pl.BlockSpec((nq, PEER_N_KEYS, PEER_HALF), lambda i: (0, 0, 0))],
        out_specs=[pl.BlockSpec((tm, PEER_SLOTS), lambda i: (i, 0))] * 2,
        out_shape=[jax.ShapeDtypeStruct((T, PEER_SLOTS), jnp.int32),
                   jax.ShapeDtypeStruct((T, PEER_SLOTS), jnp.float32)],
        scratch_shapes=[pltpu.VMEM((PEER_N_KEYS * PEER_HEADS, tm), jnp.float32),
                        pltpu.VMEM((PEER_N_KEYS * PEER_HEADS, tm), jnp.float32),
                        pltpu.VMEM((2, PEER_TOPK) + tile, jnp.float32),
                        pltpu.VMEM((2, PEER_TOPK) + tile, jnp.int32),
                        pltpu.VMEM((len(_CANDIDATES),) + tile, jnp.float32),
                        pltpu.VMEM((len(_CANDIDATES),) + tile, jnp.int32),
                        pltpu.VMEM((PEER_TOPK,) + tile, jnp.float32),
                        pltpu.VMEM((PEER_TOPK,) + tile, jnp.int32)],
        compiler_params=pltpu.CompilerParams(dimension_semantics=("arbitrary",),
                                             vmem_limit_bytes=VMEM_LIMIT),
        name="route",
    )(h2, wq_bf16, sk_bf16)


def _pack_kernel(u_ref, v_ref, o_ref):
    words = pltpu.pack_elementwise([u_ref[...], v_ref[...]], packed_dtype=jnp.bfloat16)
    for c in range(ROW_TILES):
        o_ref[pl.ds(c, PACK_BLOCK, stride=ROW_TILES), :] = words[:, c * LANES:(c + 1) * LANES]


def _pack_table(u, v):
    E = u.shape[0]
    packed = pl.pallas_call(
        _pack_kernel,
        grid=(E // PACK_BLOCK,),
        in_specs=[pl.BlockSpec((PACK_BLOCK, D_MODEL), lambda i: (i, 0))] * 2,
        out_specs=pl.BlockSpec((PACK_BLOCK * ROW_TILES, LANES), lambda i: (i, 0)),
        out_shape=jax.ShapeDtypeStruct((E * ROW_TILES, LANES), jnp.uint32),
        compiler_params=pltpu.CompilerParams(dimension_semantics=("arbitrary",),
                                             vmem_limit_bytes=VMEM_LIMIT),
        name="pack_table",
    )(u, v)
    return packed.reshape(E, ROW_TILES, LANES)


def _expert_kernel(idx_ref, gate_ref, h_ref, x1_ref, gf_ref, uv_hbm, o_ref, *scratch):
    bufs, sem = scratch[:EXPERT_RING], scratch[EXPERT_RING]
    i = pl.program_id(0)
    n = pl.num_programs(0)
    rows = EXPERT_SUB * PEER_SLOTS

    def issue(sub):
        slot = sub % EXPERT_RING
        for r in range(rows):
            e = idx_ref[0, sub * rows + r]
            dst = bufs[slot].at[pl.ds(r * ROW_PITCH, ROW_TILES), :]
            pltpu.make_async_copy(uv_hbm.at[e], dst, sem.at[slot]).start(priority=r % 2)

    def matrices(buf):
        words = [buf[pl.ds(c, rows, stride=ROW_PITCH), :] for c in range(ROW_TILES)]

        def half(index):
            tiles = [pltpu.unpack_elementwise(w, index=index, packed_dtype=jnp.bfloat16,
                                              unpacked_dtype=jnp.float32).astype(jnp.bfloat16)
                     for w in words]
            return jnp.concatenate(tiles, axis=1)
        return half(0), half(1)

    def wait(slot):
        copied = bufs[slot].at[pl.ds(0, rows * ROW_TILES), :]
        pltpu.make_async_copy(copied, copied, sem.at[slot]).wait()

    @pl.when(i == 0)
    def _():
        for sub in range(EXPERT_AHEAD):
            issue(sub)

    row_id = lax.broadcasted_iota(jnp.int32, (EXPERT_SUB, PEER_SLOTS), 0)
    row_wide = lax.broadcasted_iota(jnp.int32, (EXPERT_SUB, rows), 0)
    blk_wide = lax.broadcasted_iota(jnp.int32, (EXPERT_SUB, rows), 1) // PEER_SLOTS
    h_all = h_ref[...].astype(jnp.float32)

    for sub in range(EXPERT_RING):
        tok = pl.ds(sub * EXPERT_SUB, EXPERT_SUB)
        buf = bufs[sub]
        wait(sub)
        issue(sub + EXPERT_AHEAD)
        h = h_all[sub * EXPERT_SUB:(sub + 1) * EXPERT_SUB].astype(jnp.bfloat16)
        u, v = matrices(buf)
        a = lax.dot_general(h, u, _NT, preferred_element_type=jnp.float32)
        act = jnp.zeros((EXPERT_SUB, PEER_SLOTS), jnp.float32)
        for c in range(EXPERT_SUB):
            act = act + jnp.where(row_id == c, a[:, c * PEER_SLOTS:(c + 1) * PEER_SLOTS], 0.0)
        wgt = gate_ref[tok, :] * _gelu_exact(act)
        wide = jnp.where(blk_wide == row_wide, jnp.concatenate([wgt] * EXPERT_SUB, axis=1), 0.0)
        y = jnp.dot(wide.astype(jnp.bfloat16), v, preferred_element_type=jnp.float32)
        o_ref[tok, :] = _rms(x1_ref[tok, :] + y, gf_ref[...])

    @pl.when(i == n - 1)
    def _():
        for sub in range(EXPERT_AHEAD):
            wait(sub)


def _experts(idx, gate, h2, x1, g_final, uv):
    T = idx.shape[0]
    tile_tokens = EXPERT_RING * EXPERT_SUB
    n = T // tile_tokens
    rows = EXPERT_SUB * PEER_SLOTS
    tile = lambda w: pl.BlockSpec((tile_tokens, w), lambda i: (i, 0))
    idx3 = idx.reshape(n, tile_tokens * PEER_SLOTS)
    idx_ext = jnp.concatenate([idx3, jnp.roll(idx3[:, :EXPERT_AHEAD * rows], -1, axis=0)], axis=1)
    buf = pltpu.VMEM((rows * ROW_PITCH, LANES), jnp.uint32)
    return pl.pallas_call(
        _expert_kernel,
        grid=(n,),
        in_specs=[pl.BlockSpec((None, 1, (tile_tokens + EXPERT_AHEAD * EXPERT_SUB) * PEER_SLOTS),
                               lambda i: (i, 0, 0), memory_space=pltpu.SMEM),
                  tile(PEER_SLOTS), tile(D_MODEL), tile(D_MODEL),
                  pl.BlockSpec((1, D_MODEL), lambda i: (0, 0)),
                  pl.BlockSpec(memory_space=pl.ANY)],
        out_specs=tile(D_MODEL),
        out_shape=jax.ShapeDtypeStruct((T, D_MODEL), jnp.float32),
        scratch_shapes=[buf] * EXPERT_RING + [pltpu.SemaphoreType.DMA((EXPERT_RING,))],
        compiler_params=pltpu.CompilerParams(dimension_semantics=("arbitrary",),
                                             vmem_limit_bytes=VMEM_LIMIT,
                                             disable_bounds_checks=True),
        name="experts",
    )(idx_ext[:, None, :], gate, h2, x1, g_final, uv)


def _layer(x, g_mix, w_in, w_pool, pool_scale, w_out, g_ffn, w_peer_q, peer_subkeys, peer_u, peer_v,
           g_out):
    B, S, D = x.shape
    T = B * S
    bf = jnp.bfloat16
    x2 = x.reshape(T, D)
    proj3 = _inproj(x2, g_mix.reshape(1, D), w_in.astype(bf)).reshape(B, S, IN_PROJ_WIDTH)

    w_bd = jnp.zeros((POOL_WIDTH, POOL_WIDTH), jnp.float32)
    for gi in range(len(POOL_WINDOWS)):
        sl = slice(gi * POOL_GROUP, (gi + 1) * POOL_GROUP)
        w_bd = w_bd.at[sl, sl].set(w_pool[gi])
    pool2 = _pool(proj3, w_bd.astype(bf), pool_scale.reshape(1, POOL_WIDTH)).reshape(T, POOL_WIDTH)

    outs, lses = [], []
    for gi in range(len(ATTN_CONFIGS)):
        o_lo, o_hi, l_lo, l_hi = _attention(proj3, gi)
        outs += [o_lo.reshape(T, LANES), o_hi.reshape(T, LANES)]
        lses += [l_lo.reshape(T, LANES), l_hi.reshape(T, LANES)]

    x1, h2 = _outproj(x2, pool2, outs, lses, w_out.astype(bf), g_ffn.reshape(1, D))
    sk = peer_subkeys.reshape(2 * PEER_HEADS, PEER_N_KEYS, PEER_HALF).astype(bf)
    idx, gate = _route(h2, w_peer_q.astype(bf), sk)
    uv = _pack_table(peer_u, peer_v)
    y = _experts(idx, gate, h2, x1, g_out.reshape(1, D), uv)
    return y.reshape(B, S, D)


def kernel(x, g_mix, w_in, w_pool, pool_scale, w_out, g_ffn, w_peer_q, peer_subkeys, peer_u, peer_v,
           g_final):
    assert g_mix.shape[0] == 1, "single-layer block"
    return _layer(x, g_mix[0], w_in[0], w_pool[0], pool_scale[0], w_out[0], g_ffn[0], w_peer_q[0],
                  peer_subkeys[0], peer_u[0], peer_v[0], g_final)
```

```python
import functools
import math

import numpy as np
import jax
import jax.numpy as jnp
from jax import lax
from jax.experimental import pallas as pl
from jax.experimental.pallas import tpu as pltpu

D_MODEL = 1024
LANES = 128
SUBLANES = 8
HEAD_DIM = 64
POOL_WINDOWS = (2, 4, 8, 16)
POOL_GROUP = 64
POOL_WIDTH = POOL_GROUP * len(POOL_WINDOWS)
ATTN_CONFIGS = ((128, 1), (512, 4), (2048, 16))
HEADS_PER_GROUP = 4
GROUP_WIDTH = HEADS_PER_GROUP * HEAD_DIM
N_ATTN_HEADS = HEADS_PER_GROUP * len(ATTN_CONFIGS)
ATTN_WIDTH = N_ATTN_HEADS * HEAD_DIM
IN_PROJ_WIDTH = POOL_WIDTH + 3 * ATTN_WIDTH
ATTN_HALF = 64
PEER_HEADS = 8
PEER_N_KEYS = 128
PEER_HALF = 128
PEER_TOPK = 16
PEER_SLOTS = PEER_HEADS * PEER_TOPK
RMS_EPS = 1e-6
NEG_INF = -1e30

VMEM_LIMIT = 48 * 1024 * 1024
ATTN_VMEM_LIMIT = 56 * 1024 * 1024
POOL_PAD_LO = 16
POOL_PAD = 48
QUERY_BLOCK = 128
EXPERT_SUB = 8
EXPERT_RING = 4
EXPERT_AHEAD = 2
ROW_TILES = D_MODEL // LANES
ROW_PITCH = 12
PACK_BLOCK = 256

_NT = (((1,), (1,)), ((), ()))


def _alibi_slopes():
    return np.asarray(2.0 ** (-8.0 * np.arange(1, N_ATTN_HEADS + 1) / N_ATTN_HEADS), dtype=np.float32)


def _rms(x, g):
    return x * lax.rsqrt(jnp.mean(x * x, axis=-1, keepdims=True) + RMS_EPS) * g


def _gelu_exact(x):
    return 0.5 * x * (1.0 + lax.erf(x * math.sqrt(0.5)))


def _inproj_kernel(x_ref, g_ref, w_ref, o_ref):
    h = _rms(x_ref[...], g_ref[...])
    o_ref[...] = jnp.dot(h.astype(jnp.bfloat16), w_ref[...], preferred_element_type=jnp.float32)


def _inproj(x2, g, w_bf16, tm=512):
    T = x2.shape[0]
    return pl.pallas_call(
        _inproj_kernel,
        grid=(T // tm,),
        in_specs=[pl.BlockSpec((tm, D_MODEL), lambda i: (i, 0)),
                  pl.BlockSpec((1, D_MODEL), lambda i: (0, 0)),
                  pl.BlockSpec((D_MODEL, IN_PROJ_WIDTH), lambda i: (0, 0))],
        out_specs=pl.BlockSpec((tm, IN_PROJ_WIDTH), lambda i: (i, 0)),
        out_shape=jax.ShapeDtypeStruct((T, IN_PROJ_WIDTH), jnp.float32),
        compiler_params=pltpu.CompilerParams(dimension_semantics=("arbitrary",),
                                             vmem_limit_bytes=VMEM_LIMIT),
        name="inproj",
    )(x2, g, w_bf16)


def _pool_kernel(p_ref, w_ref, sc_ref, o_ref, a_ref, b_ref, c_ref):
    S = p_ref.shape[1]
    p = p_ref[0]
    lo = POOL_PAD_LO
    a_ref[pl.ds(0, lo), :] = jnp.zeros((lo, POOL_WIDTH), jnp.float32)
    a_ref[pl.ds(lo + S, POOL_PAD - lo), :] = jnp.zeros((POOL_PAD - lo, POOL_WIDTH), jnp.float32)
    a_ref[pl.ds(lo, S), :] = p
    n2, n4, n8 = S + 40, S + 32, S + 24
    b_ref[pl.ds(0, n2), :] = a_ref[pl.ds(0, n2), :] + a_ref[pl.ds(1, n2), :]
    c_ref[pl.ds(0, n4), :] = b_ref[pl.ds(0, n4), :] + b_ref[pl.ds(2, n4), :]
    w2 = b_ref[pl.ds(lo - 1, S), :]
    w4 = c_ref[pl.ds(lo - 2, S), :]
    b_ref[pl.ds(0, n8), :] = c_ref[pl.ds(0, n8), :] + c_ref[pl.ds(4, n8), :]
    w8 = b_ref[pl.ds(lo - 4, S), :]
    w16 = b_ref[pl.ds(lo - 8, S), :] + b_ref[pl.ds(lo, S), :]
    lane = lax.broadcasted_iota(jnp.int32, (S, POOL_WIDTH), 1)
    t = lax.broadcasted_iota(jnp.int32, (S, POOL_WIDTH), 0)
    g0, g1, g2 = lane < POOL_GROUP, lane < 2 * POOL_GROUP, lane < 3 * POOL_GROUP
    wsum = jnp.where(g0, w2, jnp.where(g1, w4, jnp.where(g2, w8, w16)))
    half = jnp.where(g0, 1, jnp.where(g1, 2, jnp.where(g2, 4, 8)))
    cnt = jnp.minimum(t + half, S) - jnp.maximum(t - half, 0)
    pooled = wsum / cnt.astype(jnp.float32) - p
    mixed = jnp.dot(pooled.astype(jnp.bfloat16), w_ref[...], preferred_element_type=jnp.float32)
    o_ref[0] = mixed * sc_ref[...]


def _pool(proj3, w_blockdiag, scale):
    B, S, _ = proj3.shape
    return pl.pallas_call(
        _pool_kernel,
        grid=(B,),
        in_specs=[pl.BlockSpec((1, S, POOL_WIDTH), lambda b: (b, 0, 0)),
                  pl.BlockSpec((POOL_WIDTH, POOL_WIDTH), lambda b: (0, 0)),
                  pl.BlockSpec((1, POOL_WIDTH), lambda b: (0, 0))],
        out_specs=pl.BlockSpec((1, S, POOL_WIDTH), lambda b: (b, 0, 0)),
        out_shape=jax.ShapeDtypeStruct((B, S, POOL_WIDTH), jnp.float32),
        scratch_shapes=[pltpu.VMEM((S + POOL_PAD, POOL_WIDTH), jnp.float32)] * 3,
        compiler_params=pltpu.CompilerParams(dimension_semantics=("arbitrary",),
                                             vmem_limit_bytes=VMEM_LIMIT),
        name="pool",
    )(proj3, w_blockdiag, scale)


def _attn_group(q0, q1, k0, k1, v0, v1, o0, o1, l0, l1, qs, ks, vs, os_, ls, *, dilation, slopes):
    S = q0.shape[1]
    d = dilation
    L = S // d
    QB = min(QUERY_BLOCK, L)
    W = min(L, QB + 2 * ATTN_HALF)
    nb = L // QB
    for r in range(d):
        rows = pl.ds(r, L, stride=d) if d > 1 else pl.ds(0, L)
        for dst, halves in ((qs, (q0, q1)), (ks, (k0, k1)), (vs, (v0, v1))):
            for c, src in enumerate(halves):
                dst[pl.ds(r * L, L), pl.ds(c * LANES, LANES)] = src[0, rows, :].astype(jnp.bfloat16)

    H = HEADS_PER_GROUP
    head_of_lane = lax.broadcasted_iota(jnp.int32, (QB, GROUP_WIDTH), 1) // HEAD_DIM
    stacked_row = lax.broadcasted_iota(jnp.int32, (H * QB, W), 0)
    col_minus_row = lax.broadcasted_iota(jnp.int32, (H * QB, W), 1) - stacked_row % QB
    head_of_row = lax.broadcasted_iota(jnp.int32, (H * QB, 1), 0) // QB
    slope_of_row = functools.reduce(
        lambda acc, h: jnp.where(head_of_row == h, slopes[h], acc), range(1, H),
        jnp.full((H * QB, 1), slopes[0], jnp.float32))
    scale = 1.0 / math.sqrt(HEAD_DIM)

    def block(n, carry):
        base = (n // nb) * L
        i0 = (n % nb) * QB
        ws = jnp.clip(i0 - ATTN_HALF, 0, L - W)
        q_rows = pl.ds(pl.multiple_of(base + i0, QB), QB)
        k_rows = pl.ds(pl.multiple_of(base + ws, ATTN_HALF), W)
        q = qs[q_rows, :]
        k = ks[k_rows, :]
        v = vs[k_rows, :]
        arel = jnp.abs(col_minus_row + (ws - i0))
        dist = (d * arel).astype(jnp.float32)
        q_stacked = jnp.concatenate(
            [jnp.where(head_of_lane == h, q, jnp.zeros_like(q)) for h in range(H)], axis=0)
        s = lax.dot_general(q_stacked, k, _NT, preferred_element_type=jnp.float32) * scale
        s = jnp.where(arel <= ATTN_HALF, s - slope_of_row * dist, NEG_INF)
        m = jnp.max(s, axis=-1, keepdims=True)
        pexp = jnp.exp(s - m)
        den = jnp.sum(pexp, axis=-1, keepdims=True)
        lse = m + jnp.log(den)
        o_stacked = jnp.dot(pexp.astype(jnp.bfloat16), v, preferred_element_type=jnp.float32) / den
        o_acc = o_stacked[0:QB]
        l_acc = jnp.broadcast_to(lse[0:QB], (QB, GROUP_WIDTH))
        for h in range(1, H):
            mine = head_of_lane == h
            o_acc = jnp.where(mine, o_stacked[h * QB:(h + 1) * QB], o_acc)
            l_acc = jnp.where(mine, lse[h * QB:(h + 1) * QB], l_acc)
        os_[q_rows, :] = o_acc
        ls[q_rows, :] = l_acc
        return carry

    lax.fori_loop(0, d * nb, block, 0)
    for r in range(d):
        rows = pl.ds(r, L, stride=d) if d > 1 else pl.ds(0, L)
        for c, (o_ref, l_ref) in enumerate(((o0, l0), (o1, l1))):
            o_ref[rows, :] = os_[pl.ds(r * L, L), pl.ds(c * LANES, LANES)]
            l_ref[rows, :] = ls[pl.ds(r * L, L), pl.ds(c * LANES, LANES)]


def _attn_kernel(*refs):
    ng = len(ATTN_CONFIGS)
    per_group = GROUP_WIDTH // LANES
    n_in = 3 * per_group * ng
    ins, attn_ref = refs[:n_in], refs[n_in]
    qs, ks, vs, os_, ls = refs[n_in + 1:n_in + 6]
    o_nat = refs[n_in + 6:n_in + 6 + ng * per_group]
    l_nat = refs[n_in + 6 + ng * per_group:]
    slopes = _alibi_slopes()
    for g, (_, d) in enumerate(ATTN_CONFIGS):
        _attn_group(*ins[3 * per_group * g:3 * per_group * (g + 1)],
                    o_nat[per_group * g], o_nat[per_group * g + 1],
                    l_nat[per_group * g], l_nat[per_group * g + 1], qs, ks, vs, os_, ls, dilation=d,
                    slopes=tuple(float(x) for x in slopes[g * HEADS_PER_GROUP:(g + 1) * HEADS_PER_GROUP]))
    for c in range(per_group):
        lse = [l_nat[per_group * g + c][...] for g in range(ng)]
        m = functools.reduce(jnp.maximum, lse)
        e = [jnp.exp(l - m) for l in lse]
        tot = functools.reduce(jnp.add, e)
        for g in range(ng):
            part = o_nat[per_group * g + c][...] * (e[g] / tot)
            attn_ref[0, :, pl.ds(g * GROUP_WIDTH + c * LANES, LANES)] = part.astype(jnp.bfloat16)


def _attention(proj3):
    B, S, _ = proj3.shape
    ng = len(ATTN_CONFIGS)
    per_group = GROUP_WIDTH // LANES
    step = ng * per_group
    cols = [POOL_WIDTH // LANES + g * per_group + t * step + c
            for g in range(ng) for t in range(3) for c in range(per_group)]
    spec = lambda cb: pl.BlockSpec((1, S, LANES), lambda b: (b, 0, cb), pipeline_mode=pl.Buffered(1))
    return pl.pallas_call(
        _attn_kernel,
        grid=(B,),
        in_specs=[spec(cb) for cb in cols],
        out_specs=pl.BlockSpec((1, S, ATTN_WIDTH), lambda b: (b, 0, 0)),
        out_shape=jax.ShapeDtypeStruct((B, S, ATTN_WIDTH), jnp.bfloat16),
        scratch_shapes=[pltpu.VMEM((S, GROUP_WIDTH), jnp.bfloat16)] * 3
                       + [pltpu.VMEM((S, GROUP_WIDTH), jnp.float32)] * 2
                       + [pltpu.VMEM((S, LANES), jnp.float32)] * (2 * ng * per_group),
        compiler_params=pltpu.CompilerParams(dimension_semantics=("arbitrary",),
                                             vmem_limit_bytes=ATTN_VMEM_LIMIT),
        name="attention",
    )(*([proj3] * len(cols)))


def _outproj_kernel(x_ref, pool_ref, attn_ref, w_ref, g_ref, x1_ref, h_ref):
    acc = x_ref[...] + jnp.dot(pool_ref[...].astype(jnp.bfloat16), w_ref[pl.ds(0, POOL_WIDTH), :],
                               preferred_element_type=jnp.float32)
    acc = acc + jnp.dot(attn_ref[...], w_ref[pl.ds(POOL_WIDTH, ATTN_WIDTH), :],
                        preferred_element_type=jnp.float32)
    x1_ref[...] = acc
    h_ref[...] = _rms(acc, g_ref[...]).astype(jnp.bfloat16)


def _outproj(x2, pool2, attn2, w_bf16, g, tm=512):
    T = x2.shape[0]
    wide = pl.BlockSpec((tm, D_MODEL), lambda i: (i, 0))
    return pl.pallas_call(
        _outproj_kernel,
        grid=(T // tm,),
        in_specs=[wide, pl.BlockSpec((tm, POOL_WIDTH), lambda i: (i, 0)),
                  pl.BlockSpec((tm, ATTN_WIDTH), lambda i: (i, 0)),
                  pl.BlockSpec((D_MODEL, D_MODEL), lambda i: (0, 0)),
                  pl.BlockSpec((1, D_MODEL), lambda i: (0, 0))],
        out_specs=[wide, wide],
        out_shape=[jax.ShapeDtypeStruct((T, D_MODEL), jnp.float32),
                   jax.ShapeDtypeStruct((T, D_MODEL), jnp.bfloat16)],
        compiler_params=pltpu.CompilerParams(dimension_semantics=("arbitrary",),
                                             vmem_limit_bytes=VMEM_LIMIT),
        name="outproj",
    )(x2, pool2, attn2, w_bf16, g)


_CANDIDATES = tuple((a, b) for a in range(PEER_TOPK) for b in range(PEER_TOPK)
                    if (a + 1) * (b + 1) <= PEER_TOPK)


def _reduce(op, xs, ways=4):
    acc = []
    for i, x in enumerate(xs):
        if i < ways:
            acc.append(x)
        else:
            acc[i % ways] = op(acc[i % ways], x)
    while len(acc) > 1:
        acc = [op(acc[i], acc[i + 1]) if i + 1 < len(acc) else acc[i] for i in range(0, len(acc), 2)]
    return acc[0]


def _first_max(values, ways=4):
    best = []
    for r, v in values:
        if len(best) < ways:
            best.append((v, jnp.full(v.shape, r, jnp.int32)))
        else:
            m, p = best[r % ways]
            later = v > m
            best[r % ways] = (jnp.where(later, v, m), jnp.where(later, r, p))

    def merge(a, b):
        take_b = (b[0] > a[0]) | ((b[0] == a[0]) & (b[1] < a[1]))
        return jnp.where(take_b, b[0], a[0]), jnp.where(take_b, b[1], a[1])

    while len(best) > 1:
        best = [merge(best[i], best[i + 1]) if i + 1 < len(best) else best[i]
                for i in range(0, len(best), 2)]
    return best[0]


def _topk_rows(row, n, k, emit):
    def removed(pos):
        for r in range(n):
            v = jnp.where(pos == r, -jnp.inf, row(r))
            row(r, v)
            yield r, v

    def body(i, carry):
        m, pos = carry
        emit(i, m, pos)
        return _first_max(removed(pos))

    lax.fori_loop(0, k, body, _first_max((r, row(r)) for r in range(n)))


def _route_kernel(h_ref, wq_ref, sk_ref, idx_ref, gate_ref, s0_scr, s1_scr, sv_scr, si_scr, cs_scr, ci_scr,
                  ts_scr, ex_scr):
    tm = h_ref.shape[0]
    q = jnp.dot(h_ref[...], wq_ref[...], preferred_element_type=jnp.float32).astype(jnp.bfloat16)
    s_scr = (s0_scr, s1_scr)
    for hd in range(PEER_HEADS):
        for p in range(2):
            c = 2 * hd + p
            s = lax.dot_general(sk_ref[c], q[:, c * PEER_HALF:(c + 1) * PEER_HALF], _NT,
                                preferred_element_type=jnp.float32)
            s_scr[p][pl.ds(hd, PEER_N_KEYS, stride=PEER_HEADS), :] = s

    for p in range(2):
        def row(r, new=None, p=p):
            if new is None:
                return s_scr[p][pl.ds(r * PEER_HEADS, PEER_HEADS), :]
            s_scr[p][pl.ds(r * PEER_HEADS, PEER_HEADS), :] = new

        def emit(i, value, r, p=p):
            sv_scr[p, i] = value
            si_scr[p, i] = r

        _topk_rows(row, PEER_N_KEYS, PEER_TOPK, emit)

    for r, (a, b) in enumerate(_CANDIDATES):
        cs_scr[r] = sv_scr[0, a] + sv_scr[1, b]
        ci_scr[r] = si_scr[0, a] * PEER_N_KEYS + si_scr[1, b]

    def cand(r, new=None):
        if new is None:
            return cs_scr[r]
        cs_scr[r] = new

    def emit_expert(i, value, r):
        ts_scr[i] = value
        ex_scr[i] = _reduce(jnp.maximum, (jnp.where(r == c, ci_scr[c], -1) for c in range(len(_CANDIDATES))))

    _topk_rows(cand, len(_CANDIDATES), PEER_TOPK, emit_expert)

    top_s = ts_scr[...]
    e = jnp.exp(top_s - top_s[0:1])
    gate = e / jnp.sum(e, axis=0, keepdims=True)
    idx_ref[...] = ex_scr[...].reshape(PEER_SLOTS, tm).T
    gate_ref[...] = gate.reshape(PEER_SLOTS, tm).T


def _route(h2, wq_bf16, sk_bf16):
    T = h2.shape[0]
    tm = LANES
    nq = 2 * PEER_HEADS
    tile = (PEER_HEADS, tm)
    assert PEER_HEADS == SUBLANES
    return pl.pallas_call(
        _route_kernel,
        grid=(T // tm,),
        in_specs=[pl.BlockSpec((tm, D_MODEL), lambda i: (i, 0)),
                  pl.BlockSpec((D_MODEL, nq * PEER_HALF), lambda i: (0, 0)),
                  pl.BlockSpec((nq, PEER_N_KEYS, PEER_HALF), lambda i: (0, 0, 0))],
        out_specs=[pl.BlockSpec((tm, PEER_SLOTS), lambda i: (i, 0))] * 2,
        out_shape=[jax.ShapeDtypeStruct((T, PEER_SLOTS), jnp.int32),
                   jax.ShapeDtypeStruct((T, PEER_SLOTS), jnp.float32)],
        scratch_shapes=[pltpu.VMEM((PEER_N_KEYS * PEER_HEADS, tm), jnp.float32),
                        pltpu.VMEM((PEER_N_KEYS * PEER_HEADS, tm), jnp.float32),
                        pltpu.VMEM((2, PEER_TOPK) + tile, jnp.float32),
                        pltpu.VMEM((2, PEER_TOPK) + tile, jnp.int32),
                        pltpu.VMEM((len(_CANDIDATES),) + tile, jnp.float32),
                        pltpu.VMEM((len(_CANDIDATES),) + tile, jnp.int32),
                        pltpu.VMEM((PEER_TOPK,) + tile, jnp.float32),
                        pltpu.VMEM((PEER_TOPK,) + tile, jnp.int32)],
        compiler_params=pltpu.CompilerParams(dimension_semantics=("arbitrary",),
                                             vmem_limit_bytes=VMEM_LIMIT),
        name="route",
    )(h2, wq_bf16, sk_bf16)


def _pack_kernel(u_ref, v_ref, o_ref):
    words = pltpu.pack_elementwise([u_ref[...], v_ref[...]], packed_dtype=jnp.bfloat16)
    for c in range(ROW_TILES):
        o_ref[pl.ds(c, PACK_BLOCK, stride=ROW_TILES), :] = words[:, c * LANES:(c + 1) * LANES]


def _pack_table(u, v):
    E = u.shape[0]
    packed = pl.pallas_call(
        _pack_kernel,
        grid=(E // PACK_BLOCK,),
        in_specs=[pl.BlockSpec((PACK_BLOCK, D_MODEL), lambda i: (i, 0))] * 2,
        out_specs=pl.BlockSpec((PACK_BLOCK * ROW_TILES, LANES), lambda i: (i, 0)),
        out_shape=jax.ShapeDtypeStruct((E * ROW_TILES, LANES), jnp.uint32),
        compiler_params=pltpu.CompilerParams(dimension_semantics=("arbitrary",),
                                             vmem_limit_bytes=VMEM_LIMIT),
        name="pack_table",
    )(u, v)
    return packed.reshape(E, ROW_TILES, LANES)


def _expert_kernel(idx_ref, gate_ref, h_ref, x1_ref, gf_ref, uv_hbm, o_ref, *scratch):
    bufs, sem = scratch[:EXPERT_RING], scratch[EXPERT_RING]
    i = pl.program_id(0)
    n = pl.num_programs(0)
    rows = EXPERT_SUB * PEER_SLOTS

    def issue(sub):
        slot = sub % EXPERT_RING
        for r in range(rows):
            e = idx_ref[0, sub * rows + r]
            dst = bufs[slot].at[pl.ds(r * ROW_PITCH, ROW_TILES), :]
            pltpu.make_async_copy(uv_hbm.at[e], dst, sem.at[slot]).start(priority=r % 2)

    def matrices(buf):
        words = [buf[pl.ds(c, rows, stride=ROW_PITCH), :] for c in range(ROW_TILES)]

        def half(index):
            tiles = [pltpu.unpack_elementwise(w, index=index, packed_dtype=jnp.bfloat16,
                                              unpacked_dtype=jnp.float32).astype(jnp.bfloat16)
                     for w in words]
            return jnp.concatenate(tiles, axis=1)
        return half(0), half(1)

    def wait(slot):
        copied = bufs[slot].at[pl.ds(0, rows * ROW_TILES), :]
        pltpu.make_async_copy(copied, copied, sem.at[slot]).wait()

    @pl.when(i == 0)
    def _():
        for sub in range(EXPERT_AHEAD):
            issue(sub)

    row_id = lax.broadcasted_iota(jnp.int32, (EXPERT_SUB, PEER_SLOTS), 0)
    row_wide = lax.broadcasted_iota(jnp.int32, (EXPERT_SUB, rows), 0)
    blk_wide = lax.broadcasted_iota(jnp.int32, (EXPERT_SUB, rows), 1) // PEER_SLOTS
    h_all = h_ref[...].astype(jnp.float32)

    for sub in range(EXPERT_RING):
        tok = pl.ds(sub * EXPERT_SUB, EXPERT_SUB)
        buf = bufs[sub]
        wait(sub)
        issue(sub + EXPERT_AHEAD)
        h = h_all[sub * EXPERT_SUB:(sub + 1) * EXPERT_SUB].astype(jnp.bfloat16)
        u, v = matrices(buf)
        a = lax.dot_general(h, u, _NT, preferred_element_type=jnp.float32)
        act = jnp.zeros((EXPERT_SUB, PEER_SLOTS), jnp.float32)
        for c in range(EXPERT_SUB):
            act = act + jnp.where(row_id == c, a[:, c * PEER_SLOTS:(c + 1) * PEER_SLOTS], 0.0)
        wgt = gate_ref[tok, :] * _gelu_exact(act)
        wide = jnp.where(blk_wide == row_wide, jnp.concatenate([wgt] * EXPERT_SUB, axis=1), 0.0)
        y = jnp.dot(wide.astype(jnp.bfloat16), v, preferred_element_type=jnp.float32)
        o_ref[tok, :] = _rms(x1_ref[tok, :] + y, gf_ref[...])

    @pl.when(i == n - 1)
    def _():
        for sub in range(EXPERT_AHEAD):
            wait(sub)


def _experts(idx, gate, h2, x1, g_final, uv):
    T = idx.shape[0]
    tile_tokens = EXPERT_RING * EXPERT_SUB
    n = T // tile_tokens
    rows = EXPERT_SUB * PEER_SLOTS
    tile = lambda w: pl.BlockSpec((tile_tokens, w), lambda i: (i, 0))
    idx3 = idx.reshape(n, tile_tokens * PEER_SLOTS)
    idx_ext = jnp.concatenate([idx3, jnp.roll(idx3[:, :EXPERT_AHEAD * rows], -1, axis=0)], axis=1)
    buf = pltpu.VMEM((rows * ROW_PITCH, LANES), jnp.uint32)
    return pl.pallas_call(
        _expert_kernel,
        grid=(n,),
        in_specs=[pl.BlockSpec((None, 1, (tile_tokens + EXPERT_AHEAD * EXPERT_SUB) * PEER_SLOTS),
                               lambda i: (i, 0, 0), memory_space=pltpu.SMEM),
                  tile(PEER_SLOTS), tile(D_MODEL), tile(D_MODEL),
                  pl.BlockSpec((1, D_MODEL), lambda i: (0, 0)),
                  pl.BlockSpec(memory_space=pl.ANY)],
        out_specs=tile(D_MODEL),
        out_shape=jax.ShapeDtypeStruct((T, D_MODEL), jnp.float32),
        scratch_shapes=[buf] * EXPERT_RING + [pltpu.SemaphoreType.DMA((EXPERT_RING,))],
        compiler_params=pltpu.CompilerParams(dimension_semantics=("arbitrary",),
                                             vmem_limit_bytes=VMEM_LIMIT,
                                             disable_bounds_checks=True),
        name="experts",
    )(idx_ext[:, None, :], gate, h2, x1, g_final, uv)


def _layer(x, g_mix, w_in, w_pool, pool_scale, w_out, g_ffn, w_peer_q, peer_subkeys, peer_u, peer_v,
           g_out):
    B, S, D = x.shape
    T = B * S
    bf = jnp.bfloat16
    x2 = x.reshape(T, D)
    proj3 = _inproj(x2, g_mix.reshape(1, D), w_in.astype(bf)).reshape(B, S, IN_PROJ_WIDTH)

    w_bd = jnp.zeros((POOL_WIDTH, POOL_WIDTH), jnp.float32)
    for gi in range(len(POOL_WINDOWS)):
        sl = slice(gi * POOL_GROUP, (gi + 1) * POOL_GROUP)
        w_bd = w_bd.at[sl, sl].set(w_pool[gi])
    pool2 = _pool(proj3, w_bd.astype(bf), pool_scale.reshape(1, POOL_WIDTH)).reshape(T, POOL_WIDTH)

    attn2 = _attention(proj3).reshape(T, ATTN_WIDTH)
    x1, h2 = _outproj(x2, pool2, attn2, w_out.astype(bf), g_ffn.reshape(1, D))
    sk = peer_subkeys.reshape(2 * PEER_HEADS, PEER_N_KEYS, PEER_HALF).astype(bf)
    idx, gate = _route(h2, w_peer_q.astype(bf), sk)
    uv = _pack_table(peer_u, peer_v)
    y = _experts(idx, gate, h2, x1, g_out.reshape(1, D), uv)
    return y.reshape(B, S, D)


def kernel(x, g_mix, w_in, w_pool, pool_scale, w_out, g_ffn, w_peer_q, peer_subkeys, peer_u, peer_v,
           g_final):
    assert g_mix.shape[0] == 1, "single-layer block"
    return _layer(x, g_mix[0], w_in[0], w_pool[0], pool_scale[0], w_out[0], g_ffn[0], w_peer_q[0],
                  peer_subkeys[0], peer_u[0], peer_v[0], g_final)
```

```python
import functools
import math

import numpy as np
import jax
import jax.numpy as jnp
from jax import lax
from jax.experimental import pallas as pl
from jax.experimental.pallas import tpu as pltpu

D_MODEL = 1024
LANES = 128
SUBLANES = 8
HEAD_DIM = 64
POOL_WINDOWS = (2, 4, 8, 16)
POOL_GROUP = 64
POOL_WIDTH = POOL_GROUP * len(POOL_WINDOWS)
ATTN_CONFIGS = ((128, 1), (512, 4), (2048, 16))
HEADS_PER_GROUP = 4
GROUP_WIDTH = HEADS_PER_GROUP * HEAD_DIM
N_ATTN_HEADS = HEADS_PER_GROUP * len(ATTN_CONFIGS)
ATTN_WIDTH = N_ATTN_HEADS * HEAD_DIM
IN_PROJ_WIDTH = POOL_WIDTH + 3 * ATTN_WIDTH
ATTN_HALF = 64
PEER_HEADS = 8
PEER_N_KEYS = 128
PEER_HALF = 128
PEER_TOPK = 16
PEER_SLOTS = PEER_HEADS * PEER_TOPK
RMS_EPS = 1e-6
NEG_INF = -1e30

VMEM_LIMIT = 48 * 1024 * 1024
POOL_PAD_LO = 16
POOL_PAD = 48
QUERY_BLOCK = 128
EXPERT_SUB = 8
EXPERT_RING = 4
EXPERT_AHEAD = 2
ROW_TILES = D_MODEL // LANES
ROW_PITCH = 12
PACK_BLOCK = 256

_NT = (((1,), (1,)), ((), ()))


def _alibi_slopes():
    return np.asarray(2.0 ** (-8.0 * np.arange(1, N_ATTN_HEADS + 1) / N_ATTN_HEADS), dtype=np.float32)


def _rms(x, g):
    return x * lax.rsqrt(jnp.mean(x * x, axis=-1, keepdims=True) + RMS_EPS) * g


def _gelu_exact(x):
    return 0.5 * x * (1.0 + lax.erf(x * math.sqrt(0.5)))


def _inproj_kernel(x_ref, g_ref, w_ref, o_ref):
    h = _rms(x_ref[...], g_ref[...])
    o_ref[...] = jnp.dot(h.astype(jnp.bfloat16), w_ref[...], preferred_element_type=jnp.float32)


def _inproj(x2, g, w_bf16, tm=512):
    T = x2.shape[0]
    return pl.pallas_call(
        _inproj_kernel,
        grid=(T // tm,),
        in_specs=[pl.BlockSpec((tm, D_MODEL), lambda i: (i, 0)),
                  pl.BlockSpec((1, D_MODEL), lambda i: (0, 0)),
                  pl.BlockSpec((D_MODEL, IN_PROJ_WIDTH), lambda i: (0, 0))],
        out_specs=pl.BlockSpec((tm, IN_PROJ_WIDTH), lambda i: (i, 0)),
        out_shape=jax.ShapeDtypeStruct((T, IN_PROJ_WIDTH), jnp.float32),
        compiler_params=pltpu.CompilerParams(dimension_semantics=("arbitrary",),
                                             vmem_limit_bytes=VMEM_LIMIT),
        name="inproj",
    )(x2, g, w_bf16)


def _pool_kernel(p_ref, w_ref, sc_ref, o_ref, a_ref, b_ref, c_ref):
    S = p_ref.shape[1]
    p = p_ref[0]
    lo = POOL_PAD_LO
    a_ref[pl.ds(0, lo), :] = jnp.zeros((lo, POOL_WIDTH), jnp.float32)
    a_ref[pl.ds(lo + S, POOL_PAD - lo), :] = jnp.zeros((POOL_PAD - lo, POOL_WIDTH), jnp.float32)
    a_ref[pl.ds(lo, S), :] = p
    n2, n4, n8 = S + 40, S + 32, S + 24
    b_ref[pl.ds(0, n2), :] = a_ref[pl.ds(0, n2), :] + a_ref[pl.ds(1, n2), :]
    c_ref[pl.ds(0, n4), :] = b_ref[pl.ds(0, n4), :] + b_ref[pl.ds(2, n4), :]
    w2 = b_ref[pl.ds(lo - 1, S), :]
    w4 = c_ref[pl.ds(lo - 2, S), :]
    b_ref[pl.ds(0, n8), :] = c_ref[pl.ds(0, n8), :] + c_ref[pl.ds(4, n8), :]
    w8 = b_ref[pl.ds(lo - 4, S), :]
    w16 = b_ref[pl.ds(lo - 8, S), :] + b_ref[pl.ds(lo, S), :]
    lane = lax.broadcasted_iota(jnp.int32, (S, POOL_WIDTH), 1)
    t = lax.broadcasted_iota(jnp.int32, (S, POOL_WIDTH), 0)
    g0, g1, g2 = lane < POOL_GROUP, lane < 2 * POOL_GROUP, lane < 3 * POOL_GROUP
    wsum = jnp.where(g0, w2, jnp.where(g1, w4, jnp.where(g2, w8, w16)))
    half = jnp.where(g0, 1, jnp.where(g1, 2, jnp.where(g2, 4, 8)))
    cnt = jnp.minimum(t + half, S) - jnp.maximum(t - half, 0)
    pooled = wsum / cnt.astype(jnp.float32) - p
    mixed = jnp.dot(pooled.astype(jnp.bfloat16), w_ref[...], preferred_element_type=jnp.float32)
    o_ref[0] = mixed * sc_ref[...]


def _pool(proj3, w_blockdiag, scale):
    B, S, _ = proj3.shape
    return pl.pallas_call(
        _pool_kernel,
        grid=(B,),
        in_specs=[pl.BlockSpec((1, S, POOL_WIDTH), lambda b: (b, 0, 0)),
                  pl.BlockSpec((POOL_WIDTH, POOL_WIDTH), lambda b: (0, 0)),
                  pl.BlockSpec((1, POOL_WIDTH), lambda b: (0, 0))],
        out_specs=pl.BlockSpec((1, S, POOL_WIDTH), lambda b: (b, 0, 0)),
        out_shape=jax.ShapeDtypeStruct((B, S, POOL_WIDTH), jnp.float32),
        scratch_shapes=[pltpu.VMEM((S + POOL_PAD, POOL_WIDTH), jnp.float32)] * 3,
        compiler_params=pltpu.CompilerParams(dimension_semantics=("arbitrary",),
                                             vmem_limit_bytes=VMEM_LIMIT),
        name="pool",
    )(proj3, w_blockdiag, scale)


def _attn_kernel(q0, q1, k0, k1, v0, v1, o0, o1, l0, l1, qs, ks, vs, os_, ls, *, dilation, slopes):
    S = q0.shape[1]
    d = dilation
    L = S // d
    QB = min(QUERY_BLOCK, L)
    W = min(L, QB + 2 * ATTN_HALF)
    nb = L // QB
    for r in range(d):
        rows = pl.ds(r, L, stride=d) if d > 1 else pl.ds(0, L)
        for dst, halves in ((qs, (q0, q1)), (ks, (k0, k1)), (vs, (v0, v1))):
            for c, src in enumerate(halves):
                dst[r, :, pl.ds(c * LANES, LANES)] = src[0, rows, :].astype(jnp.bfloat16)

    H = HEADS_PER_GROUP
    head_of_lane = lax.broadcasted_iota(jnp.int32, (QB, GROUP_WIDTH), 1) // HEAD_DIM
    stacked_row = lax.broadcasted_iota(jnp.int32, (H * QB, W), 0)
    col_minus_row = lax.broadcasted_iota(jnp.int32, (H * QB, W), 1) - stacked_row % QB
    head_of_row = lax.broadcasted_iota(jnp.int32, (H * QB, 1), 0) // QB
    slope_of_row = functools.reduce(
        lambda acc, h: jnp.where(head_of_row == h, slopes[h], acc), range(1, H),
        jnp.full((H * QB, 1), slopes[0], jnp.float32))
    scale = 1.0 / math.sqrt(HEAD_DIM)

    def block(n, carry):
        r = n // nb
        i0 = pl.multiple_of((n % nb) * QB, QB)
        ws = pl.multiple_of(jnp.clip(i0 - ATTN_HALF, 0, L - W), ATTN_HALF)
        q = qs[r, pl.ds(i0, QB), :]
        k = ks[r, pl.ds(ws, W), :]
        v = vs[r, pl.ds(ws, W), :]
        arel = jnp.abs(col_minus_row + (ws - i0))
        dist = (d * arel).astype(jnp.float32)
        q_stacked = jnp.concatenate(
            [jnp.where(head_of_lane == h, q, jnp.zeros_like(q)) for h in range(H)], axis=0)
        s = lax.dot_general(q_stacked, k, _NT, preferred_element_type=jnp.float32) * scale
        s = jnp.where(arel <= ATTN_HALF, s - slope_of_row * dist, NEG_INF)
        m = jnp.max(s, axis=-1, keepdims=True)
        pexp = jnp.exp(s - m)
        den = jnp.sum(pexp, axis=-1, keepdims=True)
        lse = m + jnp.log(den)
        o_stacked = jnp.dot(pexp.astype(jnp.bfloat16), v, preferred_element_type=jnp.float32) / den
        o_acc = o_stacked[0:QB]
        l_acc = jnp.broadcast_to(lse[0:QB], (QB, GROUP_WIDTH))
        for h in range(1, H):
            mine = head_of_lane == h
            o_acc = jnp.where(mine, o_stacked[h * QB:(h + 1) * QB], o_acc)
            l_acc = jnp.where(mine, lse[h * QB:(h + 1) * QB], l_acc)
        os_[r, pl.ds(i0, QB), :] = o_acc
        ls[r, pl.ds(i0, QB), :] = l_acc
        return carry

    lax.fori_loop(0, d * nb, block, 0)
    for r in range(d):
        rows = pl.ds(r, L, stride=d) if d > 1 else pl.ds(0, L)
        for c, (o_ref, l_ref) in enumerate(((o0, l0), (o1, l1))):
            o_ref[0, rows, :] = os_[r, :, pl.ds(c * LANES, LANES)]
            l_ref[0, rows, :] = ls[r, :, pl.ds(c * LANES, LANES)]


def _attention(proj3, group):
    B, S, _ = proj3.shape
    _, d = ATTN_CONFIGS[group]
    L = S // d
    slopes = tuple(float(s) for s in _alibi_slopes()[group * HEADS_PER_GROUP:(group + 1) * HEADS_PER_GROUP])
    per_group = GROUP_WIDTH // LANES
    first = POOL_WIDTH // LANES + group * per_group
    step = len(ATTN_CONFIGS) * per_group
    spec = lambda cb: pl.BlockSpec((1, S, LANES), lambda b: (b, 0, cb))
    out_spec = pl.BlockSpec((1, S, LANES), lambda b: (b, 0, 0))
    return pl.pallas_call(
        functools.partial(_attn_kernel, dilation=d, slopes=slopes),
        grid=(B,),
        in_specs=[spec(first + t * step + c) for t in range(3) for c in range(per_group)],
        out_specs=[out_spec] * 4,
        out_shape=[jax.ShapeDtypeStruct((B, S, LANES), jnp.float32)] * 4,
        scratch_shapes=[pltpu.VMEM((d, L, GROUP_WIDTH), jnp.bfloat16)] * 3
                       + [pltpu.VMEM((d, L, GROUP_WIDTH), jnp.float32)] * 2,
        compiler_params=pltpu.CompilerParams(dimension_semantics=("arbitrary",),
                                             vmem_limit_bytes=VMEM_LIMIT),
        name=f"attn_d{d}",
    )(*([proj3] * (3 * per_group)))


def _outproj_kernel(x_ref, pool_ref, *refs):
    ng = len(ATTN_CONFIGS)
    per_group = GROUP_WIDTH // LANES
    o_refs, l_refs = refs[:ng * per_group], refs[ng * per_group:2 * ng * per_group]
    w_ref, g_ref, x1_ref, h_ref = refs[2 * ng * per_group:]
    acc = x_ref[...] + jnp.dot(pool_ref[...].astype(jnp.bfloat16), w_ref[pl.ds(0, POOL_WIDTH), :],
                               preferred_element_type=jnp.float32)
    for c in range(per_group):
        lse = [l_refs[g * per_group + c][...] for g in range(ng)]
        m = functools.reduce(jnp.maximum, lse)
        e = [jnp.exp(l - m) for l in lse]
        tot = functools.reduce(jnp.add, e)
        for g in range(ng):
            part = o_refs[g * per_group + c][...] * (e[g] / tot)
            row0 = POOL_WIDTH + g * GROUP_WIDTH + c * LANES
            acc = acc + jnp.dot(part.astype(jnp.bfloat16), w_ref[pl.ds(row0, LANES), :],
                                preferred_element_type=jnp.float32)
    x1_ref[...] = acc
    h_ref[...] = _rms(acc, g_ref[...]).astype(jnp.bfloat16)


def _outproj(x2, pool2, os_, ls_, w_bf16, g, tm=512):
    T = x2.shape[0]
    wide = pl.BlockSpec((tm, D_MODEL), lambda i: (i, 0))
    return pl.pallas_call(
        _outproj_kernel,
        grid=(T // tm,),
        in_specs=[wide, pl.BlockSpec((tm, POOL_WIDTH), lambda i: (i, 0))]
                 + [pl.BlockSpec((tm, LANES), lambda i: (i, 0))] * (len(os_) + len(ls_))
                 + [pl.BlockSpec((D_MODEL, D_MODEL), lambda i: (0, 0)),
                    pl.BlockSpec((1, D_MODEL), lambda i: (0, 0))],
        out_specs=[wide, wide],
        out_shape=[jax.ShapeDtypeStruct((T, D_MODEL), jnp.float32),
                   jax.ShapeDtypeStruct((T, D_MODEL), jnp.bfloat16)],
        compiler_params=pltpu.CompilerParams(dimension_semantics=("arbitrary",),
                                             vmem_limit_bytes=VMEM_LIMIT),
        name="outproj",
    )(x2, pool2, *os_, *ls_, w_bf16, g)


_CANDIDATES = tuple((a, b) for a in range(PEER_TOPK) for b in range(PEER_TOPK)
                    if (a + 1) * (b + 1) <= PEER_TOPK)


def _reduce(op, xs, ways=4):
    acc = []
    for i, x in enumerate(xs):
        if i < ways:
            acc.append(x)
        else:
            acc[i % ways] = op(acc[i % ways], x)
    while len(acc) > 1:
        acc = [op(acc[i], acc[i + 1]) if i + 1 < len(acc) else acc[i] for i in range(0, len(acc), 2)]
    return acc[0]


def _first_max(values, ways=4):
    best = []
    for r, v in values:
        if len(best) < ways:
            best.append((v, jnp.full(v.shape, r, jnp.int32)))
        else:
            m, p = best[r % ways]
            later = v > m
            best[r % ways] = (jnp.where(later, v, m), jnp.where(later, r, p))

    def merge(a, b):
        take_b = (b[0] > a[0]) | ((b[0] == a[0]) & (b[1] < a[1]))
        return jnp.where(take_b, b[0], a[0]), jnp.where(take_b, b[1], a[1])

    while len(best) > 1:
        best = [merge(best[i], best[i + 1]) if i + 1 < len(best) else best[i]
                for i in range(0, len(best), 2)]
    return best[0]


def _topk_rows(row, n, k, emit):
    def removed(pos):
        for r in range(n):
            v = jnp.where(pos == r, -jnp.inf, row(r))
            row(r, v)
            yield r, v

    def body(i, carry):
        m, pos = carry
        emit(i, m, pos)
        return _first_max(removed(pos))

    lax.fori_loop(0, k, body, _first_max((r, row(r)) for r in range(n)))


def _route_kernel(h_ref, wq_ref, sk_ref, idx_ref, gate_ref, s0_scr, s1_scr, sv_scr, si_scr, cs_scr, ci_scr,
                  ts_scr, ex_scr):
    tm = h_ref.shape[0]
    q = jnp.dot(h_ref[...], wq_ref[...], preferred_element_type=jnp.float32).astype(jnp.bfloat16)
    s_scr = (s0_scr, s1_scr)
    for hd in range(PEER_HEADS):
        for p in range(2):
            c = 2 * hd + p
            s = lax.dot_general(sk_ref[c], q[:, c * PEER_HALF:(c + 1) * PEER_HALF], _NT,
                                preferred_element_type=jnp.float32)
            s_scr[p][pl.ds(hd, PEER_N_KEYS, stride=PEER_HEADS), :] = s

    for p in range(2):
        def row(r, new=None, p=p):
            if new is None:
                return s_scr[p][pl.ds(r * PEER_HEADS, PEER_HEADS), :]
            s_scr[p][pl.ds(r * PEER_HEADS, PEER_HEADS), :] = new

        def emit(i, value, r, p=p):
            sv_scr[p, i] = value
            si_scr[p, i] = r

        _topk_rows(row, PEER_N_KEYS, PEER_TOPK, emit)

    for r, (a, b) in enumerate(_CANDIDATES):
        cs_scr[r] = sv_scr[0, a] + sv_scr[1, b]
        ci_scr[r] = si_scr[0, a] * PEER_N_KEYS + si_scr[1, b]

    def cand(r, new=None):
        if new is None:
            return cs_scr[r]
        cs_scr[r] = new

    def emit_expert(i, value, r):
        ts_scr[i] = value
        ex_scr[i] = _reduce(jnp.maximum, (jnp.where(r == c, ci_scr[c], -1) for c in range(len(_CANDIDATES))))

    _topk_rows(cand, len(_CANDIDATES), PEER_TOPK, emit_expert)

    top_s = ts_scr[...]
    e = jnp.exp(top_s - top_s[0:1])
    gate = e / jnp.sum(e, axis=0, keepdims=True)
    idx_ref[...] = ex_scr[...].reshape(PEER_SLOTS, tm).T
    gate_ref[...] = gate.reshape(PEER_SLOTS, tm).T


def _route(h2, wq_bf16, sk_bf16):
    T = h2.shape[0]
    tm = LANES
    nq = 2 * PEER_HEADS
    tile = (PEER_HEADS, tm)
    assert PEER_HEADS == SUBLANES
    return pl.pallas_call(
        _route_kernel,
        grid=(T // tm,),
        in_specs=[pl.BlockSpec((tm, D_MODEL), lambda i: (i, 0)),
                  pl.BlockSpec((D_MODEL, nq * PEER_HALF), lambda i: (0, 0)),
                  pl.BlockSpec((nq, PEER_N_KEYS, PEER_HALF), lambda i: (0, 0, 0))],
        out_specs=[pl.BlockSpec((tm, PEER_SLOTS), lambda i: (i, 0))] * 2,
        out_shape=[jax.ShapeDtypeStruct((T, PEER_SLOTS), jnp.int32),
                   jax.ShapeDtypeStruct((T, PEER_SLOTS), jnp.float32)],
        scratch_shapes=[pltpu.VMEM((PEER_N_KEYS * PEER_HEADS, tm), jnp.float32),
                        pltpu.VMEM((PEER_N_KEYS * PEER_HEADS, tm), jnp.float32),
                        pltpu.VMEM((2, PEER_TOPK) + tile, jnp.float32),
                        pltpu.VMEM((2, PEER_TOPK) + tile, jnp.int32),
                        pltpu.VMEM((len(_CANDIDATES),) + tile, jnp.float32),
                        pltpu.VMEM((len(_CANDIDATES),) + tile, jnp.int32),
                        pltpu.VMEM((PEER_TOPK,) + tile, jnp.float32),
                        pltpu.VMEM((PEER_TOPK,) + tile, jnp.int32)],
        compiler_params=pltpu.CompilerParams(dimension_semantics=("arbitrary",),
                                             vmem_limit_bytes=VMEM_LIMIT),
        name="route",
    )(h2, wq_bf16, sk_bf16)


def _pack_kernel(u_ref, v_ref, o_ref):
    words = pltpu.pack_elementwise([u_ref[...], v_ref[...]], packed_dtype=jnp.bfloat16)
    for c in range(ROW_TILES):
        o_ref[pl.ds(c, PACK_BLOCK, stride=ROW_TILES), :] = words[:, c * LANES:(c + 1) * LANES]


def _pack_table(u, v):
    E = u.shape[0]
    packed = pl.pallas_call(
        _pack_kernel,
        grid=(E // PACK_BLOCK,),
        in_specs=[pl.BlockSpec((PACK_BLOCK, D_MODEL), lambda i: (i, 0))] * 2,
        out_specs=pl.BlockSpec((PACK_BLOCK * ROW_TILES, LANES), lambda i: (i, 0)),
        out_shape=jax.ShapeDtypeStruct((E * ROW_TILES, LANES), jnp.uint32),
        compiler_params=pltpu.CompilerParams(dimension_semantics=("arbitrary",),
                                             vmem_limit_bytes=VMEM_LIMIT),
        name="pack_table",
    )(u, v)
    return packed.reshape(E, ROW_TILES, LANES)


def _expert_kernel(idx_ref, next_idx_ref, gate_ref, h_ref, x1_ref, gf_ref, uv_hbm, o_ref, *scratch):
    bufs, sem = scratch[:EXPERT_RING], scratch[EXPERT_RING]
    i = pl.program_id(0)
    n = pl.num_programs(0)
    rows = EXPERT_SUB * PEER_SLOTS

    def issue(sub):
        slot = sub % EXPERT_RING
        ids = idx_ref if sub < EXPERT_RING else next_idx_ref
        for r in range(rows):
            e = ids[0, slot * rows + r]
            dst = bufs[slot].at[pl.ds(r * ROW_PITCH, ROW_TILES), :]
            pltpu.make_async_copy(uv_hbm.at[e], dst, sem.at[slot]).start(priority=r % 2)

    def matrices(buf):
        words = [buf[pl.ds(c, rows, stride=ROW_PITCH), :] for c in range(ROW_TILES)]

        def half(index):
            tiles = [pltpu.unpack_elementwise(w, index=index, packed_dtype=jnp.bfloat16,
                                              unpacked_dtype=jnp.float32).astype(jnp.bfloat16)
                     for w in words]
            return jnp.concatenate(tiles, axis=1)
        return half(0), half(1)

    def wait(slot):
        copied = bufs[slot].at[pl.ds(0, rows * ROW_TILES), :]
        pltpu.make_async_copy(copied, copied, sem.at[slot]).wait()

    @pl.when(i == 0)
    def _():
        for sub in range(EXPERT_AHEAD):
            issue(sub)

    row_id = lax.broadcasted_iota(jnp.int32, (EXPERT_SUB, PEER_SLOTS), 0)
    row_wide = lax.broadcasted_iota(jnp.int32, (EXPERT_SUB, rows), 0)
    blk_wide = lax.broadcasted_iota(jnp.int32, (EXPERT_SUB, rows), 1) // PEER_SLOTS
    h_all = h_ref[...].astype(jnp.float32)

    for sub in range(EXPERT_RING):
        tok = pl.ds(sub * EXPERT_SUB, EXPERT_SUB)
        buf = bufs[sub]
        wait(sub)
        issue(sub + EXPERT_AHEAD)
        h = h_all[sub * EXPERT_SUB:(sub + 1) * EXPERT_SUB].astype(jnp.bfloat16)
        u, v = matrices(buf)
        a = lax.dot_general(h, u, _NT, preferred_element_type=jnp.float32)
        act = jnp.zeros((EXPERT_SUB, PEER_SLOTS), jnp.float32)
        for c in range(EXPERT_SUB):
            act = act + jnp.where(row_id == c, a[:, c * PEER_SLOTS:(c + 1) * PEER_SLOTS], 0.0)
        wgt = gate_ref[tok, :] * _gelu_exact(act)
        wide = jnp.where(blk_wide == row_wide, jnp.concatenate([wgt] * EXPERT_SUB, axis=1), 0.0)
        y = jnp.dot(wide.astype(jnp.bfloat16), v, preferred_element_type=jnp.float32)
        o_ref[tok, :] = _rms(x1_ref[tok, :] + y, gf_ref[...])

    @pl.when(i == n - 1)
    def _():
        for sub in range(EXPERT_AHEAD):
            wait(sub)


def _experts(idx, gate, h2, x1, g_final, uv):
    T = idx.shape[0]
    tile_tokens = EXPERT_RING * EXPERT_SUB
    n = T // tile_tokens
    rows = EXPERT_SUB * PEER_SLOTS
    tile = lambda w: pl.BlockSpec((tile_tokens, w), lambda i: (i, 0))
    idx3 = idx.reshape(n, 1, tile_tokens * PEER_SLOTS)
    ids = lambda step: pl.BlockSpec((None, 1, tile_tokens * PEER_SLOTS), lambda i: (step(i), 0, 0),
                                    memory_space=pltpu.SMEM)
    buf = pltpu.VMEM((rows * ROW_PITCH, LANES), jnp.uint32)
    return pl.pallas_call(
        _expert_kernel,
        grid=(n,),
        in_specs=[ids(lambda i: i), ids(lambda i: (i + 1) % n),
                  tile(PEER_SLOTS), tile(D_MODEL), tile(D_MODEL),
                  pl.BlockSpec((1, D_MODEL), lambda i: (0, 0)),
                  pl.BlockSpec(memory_space=pl.ANY)],
        out_specs=tile(D_MODEL),
        out_shape=jax.ShapeDtypeStruct((T, D_MODEL), jnp.float32),
        scratch_shapes=[buf] * EXPERT_RING + [pltpu.SemaphoreType.DMA((EXPERT_RING,))],
        compiler_params=pltpu.CompilerParams(dimension_semantics=("arbitrary",),
                                             vmem_limit_bytes=VMEM_LIMIT,
                                             disable_bounds_checks=True),
        name="experts",
    )(idx3, idx3, gate, h2, x1, g_final, uv)


def _layer(x, g_mix, w_in, w_pool, pool_scale, w_out, g_ffn, w_peer_q, peer_subkeys, peer_u, peer_v,
           g_out):
    B, S, D = x.shape
    T = B * S
    bf = jnp.bfloat16
    x2 = x.reshape(T, D)
    proj3 = _inproj(x2, g_mix.reshape(1, D), w_in.astype(bf)).reshape(B, S, IN_PROJ_WIDTH)

    w_bd = jnp.zeros((POOL_WIDTH, POOL_WIDTH), jnp.float32)
    for gi in range(len(POOL_WINDOWS)):
        sl = slice(gi * POOL_GROUP, (gi + 1) * POOL_GROUP)
        w_bd = w_bd.at[sl, sl].set(w_pool[gi])
    pool2 = _pool(proj3, w_bd.astype(bf), pool_scale.reshape(1, POOL_WIDTH)).reshape(T, POOL_WIDTH)

    outs, lses = [], []
    for gi in range(len(ATTN_CONFIGS)):
        o_lo, o_hi, l_lo, l_hi = _attention(proj3, gi)
        outs += [o_lo.reshape(T, LANES), o_hi.reshape(T, LANES)]
        lses += [l_lo.reshape(T, LANES), l_hi.reshape(T, LANES)]

    x1, h2 = _outproj(x2, pool2, outs, lses, w_out.astype(bf), g_ffn.reshape(1, D))
    sk = peer_subkeys.reshape(2 * PEER_HEADS, PEER_N_KEYS, PEER_HALF).astype(bf)
    idx, gate = _route(h2, w_peer_q.astype(bf), sk)
    uv = _pack_table(peer_u, peer_v)
    y = _experts(idx, gate, h2, x1, g_out.reshape(1, D), uv)
    return y.reshape(B, S, D)


def kernel(x, g_mix, w_in, w_pool, pool_scale, w_out, g_ffn, w_peer_q, peer_subkeys, peer_u, peer_v,
           g_final):
    assert g_mix.shape[0] == 1, "single-layer block"
    return _layer(x, g_mix[0], w_in[0], w_pool[0], pool_scale[0], w_out[0], g_ffn[0], w_peer_q[0],
                  peer_subkeys[0], peer_u[0], peer_v[0], g_final)
```
